```python
import math
import jax, jax.numpy as jnp
from jax import lax
import numpy as np

D_MODEL = 4096
BATCH = 4
SEQ = 2048
DEPTH = 4

CONV_WIDTH = D_MODEL // 4
CONV_K = 3
HEAD_DIM = 128
GROUPS = ((128, 1), (512, 4), (2048, 16))
HEADS_PER_GROUP = 8
N_ATTN_HEADS = HEADS_PER_GROUP * len(GROUPS)
ATTN_WIDTH = N_ATTN_HEADS * HEAD_DIM
MERGED_ATTN_WIDTH = HEADS_PER_GROUP * HEAD_DIM
BLOCK = 128
NUM_BUCKETS = 32
MAX_DISTANCE = 2048
D_FF = 2 * D_MODEL
EPS = 1e-6
NEG = -1e30

SPLIT_SIZES = (CONV_WIDTH, CONV_WIDTH, CONV_WIDTH, ATTN_WIDTH, ATTN_WIDTH, ATTN_WIDTH, D_MODEL, D_MODEL)
IN_COLS = sum(SPLIT_SIZES)
SPLIT_POINTS = tuple(int(c) for c in np.cumsum(SPLIT_SIZES)[:-1])

kernel_name = "hybrid_conv_dilated_attn_gated_trunk"


def rmsnorm(x, g):
    xf = x.astype(jnp.float32)
    inv = lax.rsqrt(jnp.mean(xf * xf, axis=-1, keepdims=True) + EPS)
    return (xf * inv * g.astype(jnp.float32)).astype(x.dtype)


def causal_dwconv3(u, w, b):
    S = u.shape[1]
    up = jnp.pad(u, ((0, 0), (CONV_K - 1, 0), (0, 0)))
    return up[:, :S] * w[0] + up[:, 1:S + 1] * w[1] + up[:, 2:S + 2] * w[2] + b


def t5_bucket(dist):
    max_exact = NUM_BUCKETS // 2
    distf = jnp.maximum(dist, 1).astype(jnp.float32)
    large = max_exact + (jnp.log(distf / max_exact) / math.log(MAX_DISTANCE / max_exact)
                         * (NUM_BUCKETS - max_exact)).astype(jnp.int32)
    large = jnp.minimum(large, NUM_BUCKETS - 1)
    return jnp.where(dist < max_exact, dist, large)


def dilated_window_attention(q, k, v, table_g, window, dilation):
    Bn, H, S, hd = q.shape
    steps = window // dilation
    assert steps <= BLOCK
    L = S // dilation
    nb = -(-L // BLOCK)
    Lp = nb * BLOCK

    def to_sub(t):
        t = t.reshape(Bn, H, L, dilation, hd).transpose(0, 1, 3, 2, 4)
        return jnp.pad(t, ((0, 0), (0, 0), (0, 0), (0, Lp - L), (0, 0)))

    qs, ks, vs = to_sub(q), to_sub(k), to_sub(v)
    qb = qs.reshape(Bn, H, dilation, nb, BLOCK, hd)

    def band(t):
        tp = jnp.pad(t, ((0, 0), (0, 0), (0, 0), (BLOCK, 0), (0, 0)))
        prev = tp[:, :, :, :Lp].reshape(Bn, H, dilation, nb, BLOCK, hd)
        cur = t.reshape(Bn, H, dilation, nb, BLOCK, hd)
        return jnp.concatenate([prev, cur], axis=-2)

    kb, vb = band(ks), band(vs)
    s = jnp.einsum('bhrnqd,bhrnkd->bhrnqk', qb, kb).astype(jnp.float32) * (HEAD_DIM ** -0.5)

    q_idx = jnp.arange(BLOCK)
    k_idx = jnp.arange(2 * BLOCK)
    delta = (q_idx[:, None] + BLOCK) - k_idx[None, :]
    in_band = (delta >= 0) & (delta <= steps)
    bias_delta = table_g[t5_bucket(jnp.arange(steps + 1) * dilation)].astype(jnp.float32)
    bias_blk = bias_delta[jnp.clip(delta, 0, steps)].transpose(2, 0, 1)
    not_front_pad = (jnp.arange(nb)[:, None, None] > 0) | (k_idx[None, None, :] >= BLOCK)
    valid = in_band[None] & not_front_pad

    s = s + bias_blk[None, :, None, None]
    s = jnp.where(valid[None, None, None], s, NEG)
    m = jnp.max(s, axis=-1, keepdims=True)
    p = jnp.exp(s - m)
    den = jnp.sum(p, axis=-1, keepdims=True)
    o = jnp.einsum('bhrnqk,bhrnkd->bhrnqd', p, vb.astype(jnp.float32)) / den
    lse = (m + jnp.log(den))[..., 0]

    o = o.reshape(Bn, H, dilation, Lp, hd)[:, :, :, :L].transpose(0, 1, 3, 2, 4).reshape(Bn, H, S, hd)
    lse = lse.reshape(Bn, H, dilation, Lp)[:, :, :, :L].transpose(0, 1, 3, 2).reshape(Bn, H, S)
    return o, lse


def mixed_dilated_attention(q, k, v, rel_bias_table):
    Bn, S, _ = q.shape
    def heads(t):
        return t.reshape(Bn, S, N_ATTN_HEADS, HEAD_DIM).transpose(0, 2, 1, 3)
    qh, kh, vh = heads(q), heads(k), heads(v)
    outs, lses = [], []
    for g, (window, dilation) in enumerate(GROUPS):
        sl = slice(g * HEADS_PER_GROUP, (g + 1) * HEADS_PER_GROUP)
        o, lse = dilated_window_attention(qh[:, sl], kh[:, sl], vh[:, sl],
                                          rel_bias_table[:, sl], window, dilation)
        outs.append(o)
        lses.append(lse)
    o = jnp.stack(outs, 0)
    w = jax.nn.softmax(jnp.stack(lses, 0), axis=0)
    o = jnp.sum(w[..., None] * o, axis=0)
    return o.transpose(0, 2, 1, 3).reshape(Bn, S, MERGED_ATTN_WIDTH).astype(q.dtype)


def setup_inputs(seed: int = 0) -> dict:
    key = jax.random.key(seed)
    ks = jax.random.split(key, 15)
    f32 = jnp.float32
    n = lambda k, shape, scale: jax.random.normal(k, shape, f32) * scale
    return {
        "x": n(ks[0], (BATCH, SEQ, D_MODEL), 1.0),
        "rel_bias_table": n(ks[1], (NUM_BUCKETS, N_ATTN_HEADS), 0.5),
        "norm_mix_g": 1.0 + n(ks[2], (DEPTH, D_MODEL), 0.05),
        "w_in": n(ks[3], (DEPTH, D_MODEL, IN_COLS), D_MODEL ** -0.5),
        "conv_a_w": n(ks[4], (DEPTH, CONV_K, CONV_WIDTH), CONV_K ** -0.5),
        "conv_a_b": n(ks[5], (DEPTH, CONV_WIDTH), 0.02),
        "w_branch_a": n(ks[6], (DEPTH, CONV_WIDTH, D_MODEL), CONV_WIDTH ** -0.5),
        "w_branch_b": n(ks[7], (DEPTH, MERGED_ATTN_WIDTH, D_MODEL), MERGED_ATTN_WIDTH ** -0.5),
        "w_o": n(ks[8], (DEPTH, D_MODEL, D_MODEL), D_MODEL ** -0.5),
        "norm_ffn_g": 1.0 + n(ks[9], (DEPTH, D_MODEL), 0.05),
        "w_up": n(ks[10], (DEPTH, D_MODEL, 2 * D_FF), D_MODEL ** -0.5),
        "conv_f_w": n(ks[11], (DEPTH, CONV_K, D_FF), CONV_K ** -0.5),
        "conv_f_b": n(ks[12], (DEPTH, D_FF), 0.02),
        "w_down": n(ks[13], (DEPTH, D_FF, D_MODEL), D_FF ** -0.5),
        "norm_final_g": 1.0 + n(ks[14], (D_MODEL,), 0.05),
    }


def reference(x, rel_bias_table, norm_mix_g, w_in, conv_a_w, conv_a_b, w_branch_a,
              w_branch_b, w_o, norm_ffn_g, w_up, conv_f_w, conv_f_b, w_down, norm_final_g):
    for l in range(DEPTH):
        h = rmsnorm(x, norm_mix_g[l])
        proj = h @ w_in[l]
        a_h, a_b, a_c, q, k, v, gate_a, gate_b = jnp.split(proj, SPLIT_POINTS, axis=-1)
        y_a = a_b * causal_dwconv3(a_c * a_h, conv_a_w[l], conv_a_b[l])
        branch_a = y_a @ w_branch_a[l]
        y_b = mixed_dilated_attention(q, k, v, rel_bias_table)
        branch_b = y_b @ w_branch_b[l]
        merged = jax.nn.sigmoid(gate_a) * branch_a + jax.nn.sigmoid(gate_b) * branch_b
        x = x + merged @ w_o[l]
        h = rmsnorm(x, norm_ffn_g[l])
        a, b = jnp.split(h @ w_up[l], 2, axis=-1)
        a = causal_dwconv3(a, conv_f_w[l], conv_f_b[l])
        x = x + (jax.nn.gelu(a) * b) @ w_down[l]
    return rmsnorm(x, norm_final_g)
```

```python
import functools
import math

import jax
import jax.numpy as jnp
import numpy as np
from jax import lax
from jax.experimental import pallas as pl
from jax.experimental.pallas import tpu as pltpu

D_MODEL = 4096
BATCH = 4
SEQ = 2048
DEPTH = 4
TOKENS = BATCH * SEQ

CONV_WIDTH = D_MODEL // 4
HEAD_DIM = 128
GROUPS = ((128, 1), (512, 4), (2048, 16))
HEADS_PER_GROUP = 8
N_ATTN_HEADS = HEADS_PER_GROUP * len(GROUPS)
ATTN_WIDTH = N_ATTN_HEADS * HEAD_DIM
MERGED_ATTN_WIDTH = HEADS_PER_GROUP * HEAD_DIM
BLOCK = 128
NUM_BUCKETS = 32
MAX_DISTANCE = 2048
D_FF = 2 * D_MODEL
EPS = 1e-6
NEG = -1e30

MAIN_COLS = 3 * CONV_WIDTH + 3 * ATTN_WIDTH
GATE_COLS = 2 * D_MODEL
Q_COL_BLOCK = 3 * CONV_WIDTH // HEAD_DIM
N_ATTN_BLOCKS = SEQ // BLOCK

VMEM_LIMIT_BYTES = 56 * 1024 * 1024

BF16 = jnp.bfloat16
F32 = jnp.float32


def _params(semantics):
    return pltpu.CompilerParams(dimension_semantics=semantics,
                                vmem_limit_bytes=VMEM_LIMIT_BYTES)


def _rmsnorm_kernel(x_ref, g_ref, o_ref):
    x = x_ref[...]
    inv = lax.rsqrt(jnp.mean(x * x, axis=-1, keepdims=True) + EPS)
    o_ref[...] = (x * inv * g_ref[...]).astype(o_ref.dtype)


def _rmsnorm(x, g, out_dtype):
    bm = 512
    return pl.pallas_call(
        _rmsnorm_kernel,
        grid=(TOKENS // bm,),
        in_specs=[pl.BlockSpec((bm, D_MODEL), lambda i: (i, 0)),
                  pl.BlockSpec((1, D_MODEL), lambda i: (0, 0))],
        out_specs=pl.BlockSpec((bm, D_MODEL), lambda i: (i, 0)),
        out_shape=jax.ShapeDtypeStruct((TOKENS, D_MODEL), out_dtype),
        compiler_params=_params(("parallel",)),
        name="rmsnorm",
    )(x, g.reshape(1, D_MODEL))


def _proj_kernel(h_ref, w_ref, o_ref, *, sigmoid):
    acc = jnp.dot(h_ref[...], w_ref[...], preferred_element_type=F32)
    if sigmoid:
        acc = jax.nn.sigmoid(acc)
    o_ref[...] = acc.astype(o_ref.dtype)


def _in_proj(h, w_in, layer, col_block0, n_cols, sigmoid, name):
    bm, bn = 1024, 1024
    return pl.pallas_call(
        functools.partial(_proj_kernel, sigmoid=sigmoid),
        grid=(n_cols // bn, TOKENS // bm),
        in_specs=[pl.BlockSpec((bm, D_MODEL), lambda j, i: (i, 0)),
                  pl.BlockSpec((None, D_MODEL, bn),
                               lambda j, i: (layer, 0, col_block0 + j))],
        out_specs=pl.BlockSpec((bm, bn), lambda j, i: (i, j)),
        out_shape=jax.ShapeDtypeStruct((TOKENS, n_cols), BF16),
        compiler_params=_params(("parallel", "parallel")),
        name=name,
    )(h, w_in)


def _attn_kernel(*refs):
    qkv_refs = refs[:9]
    bias_ref = refs[9]
    o_ref = refs[10]
    tmp_ref, qd_ref, kd_ref, vd_ref, od_ref, ld_ref, on_ref, ln_ref = refs[11:]

    q_idx = lax.broadcasted_iota(jnp.int32, (BLOCK, 2 * BLOCK), 0)
    k_idx = lax.broadcasted_iota(jnp.int32, (BLOCK, 2 * BLOCK), 1)
    delta = q_idx + BLOCK - k_idx
    in_band = (delta >= 0) & (delta <= BLOCK)
    in_band_cur = in_band & (k_idx >= BLOCK)
    scale = HEAD_DIM ** -0.5

    kd_ref[0:BLOCK, :] = jnp.zeros((BLOCK, HEAD_DIM), BF16)
    vd_ref[0:BLOCK, :] = jnp.zeros((BLOCK, HEAD_DIM), BF16)

    for g, (_, dil) in enumerate(GROUPS):
        sub_len = SEQ // dil
        blocks_per_sub = sub_len // BLOCK
        q_ref, k_ref, v_ref = qkv_refs[3 * g:3 * g + 3]

        def to_sub(src_ref, dst_ref, dst_off):
            if dil == 1:
                dst_ref[dst_off:dst_off + SEQ, :] = src_ref[...]
                return
            tmp_ref[...] = src_ref[...].astype(F32)
            for r in range(dil):
                rows = tmp_ref[pl.ds(r, sub_len, stride=dil), :]
                dst_ref[dst_off + r * sub_len:dst_off + (r + 1) * sub_len, :] = rows.astype(BF16)

        to_sub(q_ref, qd_ref, 0)
        to_sub(k_ref, kd_ref, BLOCK)
        to_sub(v_ref, vd_ref, BLOCK)

        bias = bias_ref[g]

        def block_body(blk, carry):
            row0 = pl.multiple_of(blk * BLOCK, BLOCK)
            qb = qd_ref[pl.ds(row0, BLOCK), :]
            kb = kd_ref[pl.ds(row0, 2 * BLOCK), :]
            vb = vd_ref[pl.ds(row0, 2 * BLOCK), :]
            s = lax.dot_general(qb, kb, (((1,), (1,)), ((), ())),
                                preferred_element_type=F32) * scale
            s = s + bias
            has_prev = (blk % blocks_per_sub) > 0
            valid = in_band_cur | (in_band & has_prev)
            s = jnp.where(valid, s, NEG)
            m = jnp.max(s, axis=-1, keepdims=True)
            p = jnp.exp(s - m)
            den = jnp.sum(p, axis=-1, keepdims=True)
            o = jnp.dot(p.astype(BF16), vb, preferred_element_type=F32) / den
            lse = m + jnp.log(den)
            od_ref[pl.ds(row0, BLOCK), :] = o
            ld_ref[pl.ds(row0, BLOCK), :] = jnp.broadcast_to(lse, (BLOCK, HEAD_DIM))
            return carry

        lax.fori_loop(0, N_ATTN_BLOCKS, block_body, 0)

        if dil == 1:
            on_ref[g] = od_ref[...]
            ln_ref[g] = ld_ref[...]
        else:
            for r in range(dil):
                on_ref[g, pl.ds(r, sub_len, stride=dil), :] = od_ref[r * sub_len:(r + 1) * sub_len, :]
                ln_ref[g, pl.ds(r, sub_len, stride=dil), :] = ld_ref[r * sub_len:(r + 1) * sub_len, :]

    l0, l1, l2 = ln_ref[0], ln_ref[1], ln_ref[2]
    mx = jnp.maximum(jnp.maximum(l0, l1), l2)
    e0, e1, e2 = jnp.exp(l0 - mx), jnp.exp(l1 - mx), jnp.exp(l2 - mx)
    tot = e0 + e1 + e2
    out = (e0 / tot) * on_ref[0] + (e1 / tot) * on_ref[1] + (e2 / tot) * on_ref[2]
    o_ref[...] = out.astype(o_ref.dtype)


def _attention(pm, bias):
    in_specs = []
    for g in range(len(GROUPS)):
        for part in range(3):
            col0 = Q_COL_BLOCK + part * N_ATTN_HEADS + g * HEADS_PER_GROUP
            in_specs.append(pl.BlockSpec(
                (SEQ, HEAD_DIM), lambda b, h, col0=col0: (b, col0 + h)))
    in_specs.append(pl.BlockSpec((len(GROUPS), None, BLOCK, 2 * BLOCK),
                                 lambda b, h: (0, h, 0, 0)))
    return pl.pallas_call(
        _attn_kernel,
        grid=(BATCH, HEADS_PER_GROUP),
        in_specs=in_specs,
        out_specs=pl.BlockSpec((SEQ, HEAD_DIM), lambda b, h: (b, h)),
        out_shape=jax.ShapeDtypeStruct((TOKENS, MERGED_ATTN_WIDTH), BF16),
        scratch_shapes=[
            pltpu.VMEM((SEQ, HEAD_DIM), F32),
            pltpu.VMEM((SEQ, HEAD_DIM), BF16),
            pltpu.VMEM((SEQ + BLOCK, HEAD_DIM), BF16),
            pltpu.VMEM((SEQ + BLOCK, HEAD_DIM), BF16),
            pltpu.VMEM((SEQ, HEAD_DIM), F32),
            pltpu.VMEM((SEQ, HEAD_DIM), F32),
            pltpu.VMEM((len(GROUPS), SEQ, HEAD_DIM), F32),
            pltpu.VMEM((len(GROUPS), SEQ, HEAD_DIM), F32),
        ],
        compiler_params=_params(("parallel", "parallel")),
        name="dilated_attention",
    )(*([pm] * 9), bias)


def _t5_bucket(dist):
    max_exact = NUM_BUCKETS // 2
    distf = jnp.maximum(dist, 1).astype(F32)
    large = max_exact + (jnp.log(distf / max_exact) / math.log(MAX_DISTANCE / max_exact)
                         * (NUM_BUCKETS - max_exact)).astype(jnp.int32)
    large = jnp.minimum(large, NUM_BUCKETS - 1)
    return jnp.where(dist < max_exact, dist, large)


def _bias_blocks(rel_bias_table):
    q_idx = jnp.arange(BLOCK)
    k_idx = jnp.arange(2 * BLOCK)
    delta = (q_idx[:, None] + BLOCK) - k_idx[None, :]
    out = []
    for g, (window, dil) in enumerate(GROUPS):
        steps = window // dil
        table_g = rel_bias_table[:, g * HEADS_PER_GROUP:(g + 1) * HEADS_PER_GROUP]
        bias_delta = table_g[_t5_bucket(jnp.arange(steps + 1) * dil)].astype(F32)
        out.append(bias_delta[jnp.clip(delta, 0, steps)].transpose(2, 0, 1))
    return jnp.stack(out, 0)


def _mix_kernel(ah_ref, ab_ref, ac_ref, ahp_ref, acp_ref, yb_ref, sa_ref, sb_ref,
                wa_ref, wb_ref, cw_ref, cb_ref, o_ref, ubuf_ref, ya_ref, *, bm):
    i = pl.program_id(0)
    j = pl.program_id(1)

    @pl.when(j == 0)
    def _():
        u = ac_ref[...].astype(F32) * ah_ref[...].astype(F32)
        halo = acp_ref[...].astype(F32) * ahp_ref[...].astype(F32)
        halo = jnp.where(i % (SEQ // bm) == 0, 0.0, halo)
        ubuf_ref[0:8, :] = halo[8:16, :]
        ubuf_ref[8:bm + 8, :] = u
        u1 = ubuf_ref[7:bm + 7, :]
        u2 = ubuf_ref[6:bm + 6, :]
        conv = u2 * cw_ref[0:1, :] + u1 * cw_ref[1:2, :] + u * cw_ref[2:3, :] + cb_ref[...]
        ya_ref[...] = (ab_ref[...].astype(F32) * conv).astype(BF16)

    br_a = jnp.dot(ya_ref[...], wa_ref[...], preferred_element_type=F32)
    br_b = jnp.dot(yb_ref[...], wb_ref[...], preferred_element_type=F32)
    merged = sa_ref[...].astype(F32) * br_a + sb_ref[...].astype(F32) * br_b
    o_ref[...] = merged.astype(o_ref.dtype)


def _mix(pm, yb, gates, w_a, w_b, conv_w, conv_b, layer):
    bm, bn = 512, 1024
    halo = 16
    n_j = D_MODEL // bn

    def prev_rows(i, j):
        return (jnp.maximum(i * (bm // halo) - 1, 0), 0)

    return pl.pallas_call(
        functools.partial(_mix_kernel, bm=bm),
        grid=(TOKENS // bm, n_j),
        in_specs=[
            pl.BlockSpec((bm, CONV_WIDTH), lambda i, j: (i, 0)),
            pl.BlockSpec((bm, CONV_WIDTH), lambda i, j: (i, 1)),
            pl.BlockSpec((bm, CONV_WIDTH), lambda i, j: (i, 2)),
            pl.BlockSpec((halo, CONV_WIDTH), prev_rows),
            pl.BlockSpec((halo, CONV_WIDTH),
                         lambda i, j: (jnp.maximum(i * (bm // halo) - 1, 0), 2)),
            pl.BlockSpec((bm, MERGED_ATTN_WIDTH), lambda i, j: (i, 0)),
            pl.BlockSpec((bm, bn), lambda i, j: (i, j)),
            pl.BlockSpec((bm, bn), lambda i, j: (i, n_j + j)),
            pl.BlockSpec((None, CONV_WIDTH, bn), lambda i, j: (layer, 0, j)),
            pl.BlockSpec((None, MERGED_ATTN_WIDTH, bn), lambda i, j: (layer, 0, j)),
            pl.BlockSpec((None, 3, CONV_WIDTH), lambda i, j: (layer, 0, 0)),
            pl.BlockSpec((None, 1, CONV_WIDTH), lambda i, j: (layer, 0, 0)),
        ],
        out_specs=pl.BlockSpec((bm, bn), lambda i, j: (i, j)),
        out_shape=jax.ShapeDtypeStruct((TOKENS, D_MODEL), BF16),
        scratch_shapes=[pltpu.VMEM((bm + 8, CONV_WIDTH), F32),
                        pltpu.VMEM((bm, CONV_WIDTH), BF16)],
        compiler_params=_params(("parallel", "arbitrary")),
        name="mix_branches",
    )(pm, pm, pm, pm, pm, yb, gates, gates, w_a, w_b, conv_w,
      conv_b.reshape(DEPTH, 1, CONV_WIDTH))


def _resid_kernel(a_ref, w_ref, x_ref, o_ref):
    o_ref[...] = x_ref[...] + jnp.dot(a_ref[...], w_ref[...], preferred_element_type=F32)


def _resid_proj(a, w, x, layer, bm, bn, single_buffer_w, name):
    k = a.shape[1]
    w_kwargs = dict(pipeline_mode=pl.Buffered(1)) if single_buffer_w else {}
    return pl.pallas_call(
        _resid_kernel,
        grid=(D_MODEL // bn, TOKENS // bm),
        in_specs=[pl.BlockSpec((bm, k), lambda j, i: (i, 0)),
                  pl.BlockSpec((None, k, bn), lambda j, i: (layer, 0, j), **w_kwargs),
                  pl.BlockSpec((bm, bn), lambda j, i: (i, j))],
        out_specs=pl.BlockSpec((bm, bn), lambda j, i: (i, j)),
        out_shape=jax.ShapeDtypeStruct((TOKENS, D_MODEL), F32),
        compiler_params=_params(("parallel", "parallel")),
        name=name,
    )(a, w, x)


def _ffn_up_kernel(h_ref, wa_ref, wb_ref, cw_ref, cb_ref, o_ref, abuf_ref, *, bm):
    i = pl.program_id(1)

    @pl.when(i % (SEQ // bm) == 0)
    def _():
        abuf_ref[0:8, :] = jnp.zeros((8, abuf_ref.shape[1]), F32)

    h = h_ref[...]
    a = jnp.dot(h, wa_ref[...], preferred_element_type=F32)
    abuf_ref[8:bm + 8, :] = a
    a1 = abuf_ref[7:bm + 7, :]
    a2 = abuf_ref[6:bm + 6, :]
    conv = a2 * cw_ref[0:1, :] + a1 * cw_ref[1:2, :] + a * cw_ref[2:3, :] + cb_ref[...]
    b = jnp.dot(h, wb_ref[...], preferred_element_type=F32)
    o_ref[...] = (jax.nn.gelu(conv) * b).astype(o_ref.dtype)
    abuf_ref[0:8, :] = abuf_ref[bm:bm + 8, :]


def _ffn_up(h, w_up, conv_w, conv_b, layer):
    bm, bn = 1024, 512
    n_j = D_FF // bn
    return pl.pallas_call(
        functools.partial(_ffn_up_kernel, bm=bm),
        grid=(n_j, TOKENS // bm),
        in_specs=[pl.BlockSpec((bm, D_MODEL), lambda j, i: (i, 0)),
                  pl.BlockSpec((None, D_MODEL, bn), lambda j, i: (layer, 0, j)),
                  pl.BlockSpec((None, D_MODEL, bn), lambda j, i: (layer, 0, n_j + j)),
                  pl.BlockSpec((None, 3, bn), lambda j, i: (layer, 0, j)),
                  pl.BlockSpec((None, 1, bn), lambda j, i: (layer, 0, j))],
        out_specs=pl.BlockSpec((bm, bn), lambda j, i: (i, j)),
        out_shape=jax.ShapeDtypeStruct((TOKENS, D_FF), BF16),
        scratch_shapes=[pltpu.VMEM((bm + 8, bn), F32)],
        compiler_params=_params(("parallel", "arbitrary")),
        name="ffn_up",
    )(h, w_up, w_up, conv_w, conv_b.reshape(DEPTH, 1, D_FF))


def kernel(x, rel_bias_table, norm_mix_g, w_in, conv_a_w, conv_a_b, w_branch_a, w_branch_b,
           w_o, norm_ffn_g, w_up, conv_f_w, conv_f_b, w_down, norm_final_g):
    w_in = w_in.astype(BF16)
    w_branch_a = w_branch_a.astype(BF16)
    w_branch_b = w_branch_b.astype(BF16)
    w_o = w_o.astype(BF16)
    w_up = w_up.astype(BF16)
    w_down = w_down.astype(BF16)
    bias = _bias_blocks(rel_bias_table)

    xs = x.reshape(TOKENS, D_MODEL)
    for l in range(DEPTH):
        h = _rmsnorm(xs, norm_mix_g[l], BF16)
        pm = _in_proj(h, w_in, l, 0, MAIN_COLS, False, "in_proj_main")
        gates = _in_proj(h, w_in, l, MAIN_COLS // 1024, GATE_COLS, True, "in_proj_gates")
        yb = _attention(pm, bias)
        merged = _mix(pm, yb, gates, w_branch_a, w_branch_b, conv_a_w, conv_a_b, l)
        xs = _resid_proj(merged, w_o, xs, l, 1024, 1024, False, "out_proj")
        h = _rmsnorm(xs, norm_ffn_g[l], BF16)
        g = _ffn_up(h, w_up, conv_f_w, conv_f_b, l)
        xs = _resid_proj(g, w_down, xs, l, 512, 1024, True, "ffn_down")
    out = _rmsnorm(xs, norm_final_g, F32)
    return out.reshape(BATCH, SEQ, D_MODEL)
```

```python
import functools
import math

import jax
import jax.numpy as jnp
import numpy as np
from jax import lax
from jax.experimental import pallas as pl
from jax.experimental.pallas import tpu as pltpu

D_MODEL = 4096
BATCH = 4
SEQ = 2048
DEPTH = 4
TOKENS = BATCH * SEQ

CONV_WIDTH = D_MODEL // 4
HEAD_DIM = 128
GROUPS = ((128, 1), (512, 4), (2048, 16))
HEADS_PER_GROUP = 8
N_ATTN_HEADS = HEADS_PER_GROUP * len(GROUPS)
ATTN_WIDTH = N_ATTN_HEADS * HEAD_DIM
MERGED_ATTN_WIDTH = HEADS_PER_GROUP * HEAD_DIM
BLOCK = 128
NUM_BUCKETS = 32
MAX_DISTANCE = 2048
D_FF = 2 * D_MODEL
EPS = 1e-6
NEG = -1e30

MAIN_COLS = 3 * CONV_WIDTH + 3 * ATTN_WIDTH
GATE_COLS = 2 * D_MODEL
Q_COL_BLOCK = 3 * CONV_WIDTH // HEAD_DIM
N_ATTN_BLOCKS = SEQ // BLOCK
assert all(window // dil == BLOCK for window, dil in GROUPS)

VMEM_LIMIT_BYTES = 56 * 1024 * 1024

BF16 = jnp.bfloat16
F32 = jnp.float32


def _params(semantics):
    return pltpu.CompilerParams(dimension_semantics=semantics,
                                vmem_limit_bytes=VMEM_LIMIT_BYTES)


def _rmsnorm_kernel(x_ref, g_ref, o_ref):
    x = x_ref[...]
    inv = lax.rsqrt(jnp.mean(x * x, axis=-1, keepdims=True) + EPS)
    o_ref[...] = (x * inv * g_ref[...]).astype(o_ref.dtype)


def _rmsnorm(x, g, out_dtype):
    bm = 512
    return pl.pallas_call(
        _rmsnorm_kernel,
        grid=(TOKENS // bm,),
        in_specs=[pl.BlockSpec((bm, D_MODEL), lambda i: (i, 0)),
                  pl.BlockSpec((1, D_MODEL), lambda i: (0, 0))],
        out_specs=pl.BlockSpec((bm, D_MODEL), lambda i: (i, 0)),
        out_shape=jax.ShapeDtypeStruct((TOKENS, D_MODEL), out_dtype),
        compiler_params=_params(("parallel",)),
        name="rmsnorm",
    )(x, g.reshape(1, D_MODEL))


def _proj_kernel(h_ref, w_ref, o_ref, *, sigmoid):
    acc = jnp.dot(h_ref[...], w_ref[...], preferred_element_type=F32)
    if sigmoid:
        acc = jax.nn.sigmoid(acc)
    o_ref[...] = acc.astype(o_ref.dtype)


def _in_proj(h, w_in, layer, col_block0, n_cols, sigmoid, name):
    bm, bn = 1024, 1024
    return pl.pallas_call(
        functools.partial(_proj_kernel, sigmoid=sigmoid),
        grid=(n_cols // bn, TOKENS // bm),
        in_specs=[pl.BlockSpec((bm, D_MODEL), lambda j, i: (i, 0)),
                  pl.BlockSpec((None, D_MODEL, bn),
                               lambda j, i: (layer, 0, col_block0 + j))],
        out_specs=pl.BlockSpec((bm, bn), lambda j, i: (i, j)),
        out_shape=jax.ShapeDtypeStruct((TOKENS, n_cols), BF16),
        compiler_params=_params(("parallel", "parallel")),
        name=name,
    )(h, w_in)


def _attn_kernel(*refs):
    qkv_refs = refs[:9]
    bias_ref = refs[9]
    o_ref = refs[10]
    stage_ref, stage2_ref, sub_ref, p_ref, on_ref, ln_ref = refs[11:]
    scale = HEAD_DIM ** -0.5

    row = lax.broadcasted_iota(jnp.int32, (BLOCK, BLOCK), 0)
    lane = lax.broadcasted_iota(jnp.int32, (BLOCK, BLOCK), 1)
    valid_cur = lane <= row
    valid_band = jnp.concatenate([lane >= row, valid_cur], axis=1)

    for g, (_, dil) in enumerate(GROUPS):
        sub_len = SEQ // dil
        blocks_per_sub = sub_len // BLOCK

        def to_sub(part):
            stage = stage_ref.at[g - 1, part]
            dst = sub_ref.at[g - 1, part]
            stage[...] = qkv_refs[3 * g + part][...].astype(F32)
            if dil == 4:
                for r in range(4):
                    rows = stage[pl.ds(r, sub_len, stride=4), :]
                    dst[r * sub_len:(r + 1) * sub_len, :] = rows.astype(BF16)
            else:
                stage2 = stage2_ref.at[part]
                quarter = SEQ // 4
                for b in range(4):
                    stage2[b * quarter:(b + 1) * quarter, :] = stage[pl.ds(b, quarter, stride=4), :]
                for b in range(4):
                    for a in range(4):
                        rows = stage2[pl.ds(b * quarter + a, sub_len, stride=4), :]
                        r = 4 * a + b
                        dst[r * sub_len:(r + 1) * sub_len, :] = rows.astype(BF16)
            return dst

        if dil == 1:
            qs_ref, ks_ref, vs_ref = qkv_refs[0:3]
        else:
            qs_ref, ks_ref, vs_ref = to_sub(0), to_sub(1), to_sub(2)

        for blk in range(N_ATTN_BLOCKS):
            r, n = divmod(blk, blocks_per_sub)
            row0 = blk * BLOCK
            qb = qs_ref[row0:row0 + BLOCK, :]
            if n > 0:
                kb = ks_ref[row0 - BLOCK:row0 + BLOCK, :]
                s = lax.dot_general(qb, kb, (((1,), (1,)), ((), ())),
                                    preferred_element_type=F32) * scale
                s = jnp.where(valid_band, s + bias_ref[g], NEG)
                m = jnp.max(s, axis=-1, keepdims=True)
                p = jnp.exp(s - m)
                den = jnp.sum(p, axis=-1, keepdims=True)
            else:
                kb = ks_ref[row0:row0 + BLOCK, :]
                s = lax.dot_general(qb, kb, (((1,), (1,)), ((), ())),
                                    preferred_element_type=F32) * scale
                s = jnp.where(valid_cur, s + bias_ref[g, :, BLOCK:], NEG)
                m = jnp.maximum(jnp.max(s, axis=-1, keepdims=True), NEG)
                p = jnp.exp(s - m)
                den = jnp.sum(p, axis=-1, keepdims=True) + BLOCK * jnp.exp(NEG - m)
            p_ref[g, row0:row0 + BLOCK, 0:p.shape[1]] = (p / den).astype(BF16)
            lse = jnp.broadcast_to(m + jnp.log(den), (BLOCK, HEAD_DIM))
            if dil == 1:
                ln_ref[g, row0:row0 + BLOCK, :] = lse
            else:
                ln_ref[g, pl.ds(n * BLOCK * dil + r, BLOCK, stride=dil), :] = lse

        for blk in range(N_ATTN_BLOCKS):
            r, n = divmod(blk, blocks_per_sub)
            row0 = blk * BLOCK
            if n > 0:
                o = jnp.dot(p_ref[g, row0:row0 + BLOCK, :], vs_ref[row0 - BLOCK:row0 + BLOCK, :],
                            preferred_element_type=F32)
            else:
                o = jnp.dot(p_ref[g, row0:row0 + BLOCK, 0:BLOCK], vs_ref[row0:row0 + BLOCK, :],
                            preferred_element_type=F32)
            if dil == 1:
                on_ref[g, row0:row0 + BLOCK, :] = o
            else:
                on_ref[g, pl.ds(n * BLOCK * dil + r, BLOCK, stride=dil), :] = o

    l0, l1, l2 = ln_ref[0], ln_ref[1], ln_ref[2]
    mx = jnp.maximum(jnp.maximum(l0, l1), l2)
    e0, e1, e2 = jnp.exp(l0 - mx), jnp.exp(l1 - mx), jnp.exp(l2 - mx)
    tot = e0 + e1 + e2
    out = (e0 / tot) * on_ref[0] + (e1 / tot) * on_ref[1] + (e2 / tot) * on_ref[2]
    o_ref[...] = out.astype(o_ref.dtype)


def _attention(pm, bias):
    in_specs = []
    for g in range(len(GROUPS)):
        for part in range(3):
            col0 = Q_COL_BLOCK + part * N_ATTN_HEADS + g * HEADS_PER_GROUP
            in_specs.append(pl.BlockSpec(
                (SEQ, HEAD_DIM), lambda b, h, col0=col0: (b, col0 + h)))
    in_specs.append(pl.BlockSpec((len(GROUPS), None, BLOCK, 2 * BLOCK),
                                 lambda b, h: (0, h, 0, 0)))
    return pl.pallas_call(
        _attn_kernel,
        grid=(BATCH, HEADS_PER_GROUP),
        in_specs=in_specs,
        out_specs=pl.BlockSpec((SEQ, HEAD_DIM), lambda b, h: (b, h)),
        out_shape=jax.ShapeDtypeStruct((TOKENS, MERGED_ATTN_WIDTH), BF16),
        scratch_shapes=[
            pltpu.VMEM((2, 3, SEQ, HEAD_DIM), F32),
            pltpu.VMEM((3, SEQ, HEAD_DIM), F32),
            pltpu.VMEM((2, 3, SEQ, HEAD_DIM), BF16),
            pltpu.VMEM((len(GROUPS), SEQ, 2 * BLOCK), BF16),
            pltpu.VMEM((len(GROUPS), SEQ, HEAD_DIM), F32),
            pltpu.VMEM((len(GROUPS), SEQ, HEAD_DIM), F32),
        ],
        compiler_params=_params(("parallel", "parallel")),
        name="dilated_attention",
    )(*([pm] * 9), bias)


def _t5_bucket(dist):
    max_exact = NUM_BUCKETS // 2
    distf = jnp.maximum(dist, 1).astype(F32)
    large = max_exact + (jnp.log(distf / max_exact) / math.log(MAX_DISTANCE / max_exact)
                         * (NUM_BUCKETS - max_exact)).astype(jnp.int32)
    large = jnp.minimum(large, NUM_BUCKETS - 1)
    return jnp.where(dist < max_exact, dist, large)


def _bias_blocks(rel_bias_table):
    q_idx = jnp.arange(BLOCK)
    k_idx = jnp.arange(2 * BLOCK)
    delta = (q_idx[:, None] + BLOCK) - k_idx[None, :]
    out = []
    for g, (window, dil) in enumerate(GROUPS):
        steps = window // dil
        table_g = rel_bias_table[:, g * HEADS_PER_GROUP:(g + 1) * HEADS_PER_GROUP]
        bias_delta = table_g[_t5_bucket(jnp.arange(steps + 1) * dil)].astype(F32)
        onehot = (jnp.clip(delta, 0, steps)[:, :, None] == jnp.arange(steps + 1)).astype(F32)
        out.append(jnp.einsum('qkj,jh->hqk', onehot, bias_delta, precision=lax.Precision.HIGHEST))
    return jnp.stack(out, 0)


def _mix_kernel(ah_ref, ab_ref, ac_ref, ahp_ref, acp_ref, yb_ref, sa_ref, sb_ref,
                wa_ref, wb_ref, cw_ref, cb_ref, o_ref, ubuf_ref, ya_ref, *, bm):
    i = pl.program_id(0)
    j = pl.program_id(1)

    @pl.when(j == 0)
    def _():
        u = ac_ref[...].astype(F32) * ah_ref[...].astype(F32)
        halo = acp_ref[...].astype(F32) * ahp_ref[...].astype(F32)
        halo = jnp.where(i % (SEQ // bm) == 0, 0.0, halo)
        ubuf_ref[0:8, :] = halo[8:16, :]
        ubuf_ref[8:bm + 8, :] = u
        u1 = ubuf_ref[7:bm + 7, :]
        u2 = ubuf_ref[6:bm + 6, :]
        conv = u2 * cw_ref[0:1, :] + u1 * cw_ref[1:2, :] + u * cw_ref[2:3, :] + cb_ref[...]
        ya_ref[...] = (ab_ref[...].astype(F32) * conv).astype(BF16)

    br_a = jnp.dot(ya_ref[...], wa_ref[...], preferred_element_type=F32)
    br_b = jnp.dot(yb_ref[...], wb_ref[...], preferred_element_type=F32)
    merged = sa_ref[...].astype(F32) * br_a + sb_ref[...].astype(F32) * br_b
    o_ref[...] = merged.astype(o_ref.dtype)


def _mix(pm, yb, gates, w_a, w_b, conv_w, conv_b, layer):
    bm, bn = 512, 1024
    halo = 16
    n_j = D_MODEL // bn

    def prev_rows(i, j):
        return (jnp.maximum(i * (bm // halo) - 1, 0), 0)

    return pl.pallas_call(
        functools.partial(_mix_kernel, bm=bm),
        grid=(TOKENS // bm, n_j),
        in_specs=[
            pl.BlockSpec((bm, CONV_WIDTH), lambda i, j: (i, 0)),
            pl.BlockSpec((bm, CONV_WIDTH), lambda i, j: (i, 1)),
            pl.BlockSpec((bm, CONV_WIDTH), lambda i, j: (i, 2)),
            pl.BlockSpec((halo, CONV_WIDTH), prev_rows),
            pl.BlockSpec((halo, CONV_WIDTH),
                         lambda i, j: (jnp.maximum(i * (bm // halo) - 1, 0), 2)),
            pl.BlockSpec((bm, MERGED_ATTN_WIDTH), lambda i, j: (i, 0)),
            pl.BlockSpec((bm, bn), lambda i, j: (i, j)),
            pl.BlockSpec((bm, bn), lambda i, j: (i, n_j + j)),
            pl.BlockSpec((None, CONV_WIDTH, bn), lambda i, j: (layer, 0, j)),
            pl.BlockSpec((None, MERGED_ATTN_WIDTH, bn), lambda i, j: (layer, 0, j)),
            pl.BlockSpec((None, 3, CONV_WIDTH), lambda i, j: (layer, 0, 0)),
            pl.BlockSpec((None, 1, CONV_WIDTH), lambda i, j: (layer, 0, 0)),
        ],
        out_specs=pl.BlockSpec((bm, bn), lambda i, j: (i, j)),
        out_shape=jax.ShapeDtypeStruct((TOKENS, D_MODEL), BF16),
        scratch_shapes=[pltpu.VMEM((bm + 8, CONV_WIDTH), F32),
                        pltpu.VMEM((bm, CONV_WIDTH), BF16)],
        compiler_params=_params(("parallel", "arbitrary")),
        name="mix_branches",
    )(pm, pm, pm, pm, pm, yb, gates, gates, w_a, w_b, conv_w,
      conv_b.reshape(DEPTH, 1, CONV_WIDTH))


def _resid_kernel(a_ref, w_ref, x_ref, o_ref):
    o_ref[...] = x_ref[...] + jnp.dot(a_ref[...], w_ref[...], preferred_element_type=F32)


def _resid_proj(a, w, x, layer, bm, bn, single_buffer_w, name):
    k = a.shape[1]
    w_kwargs = dict(pipeline_mode=pl.Buffered(1)) if single_buffer_w else {}
    return pl.pallas_call(
        _resid_kernel,
        grid=(D_MODEL // bn, TOKENS // bm),
        in_specs=[pl.BlockSpec((bm, k), lambda j, i: (i, 0)),
                  pl.BlockSpec((None, k, bn), lambda j, i: (layer, 0, j), **w_kwargs),
                  pl.BlockSpec((bm, bn), lambda j, i: (i, j))],
        out_specs=pl.BlockSpec((bm, bn), lambda j, i: (i, j)),
        out_shape=jax.ShapeDtypeStruct((TOKENS, D_MODEL), F32),
        compiler_params=_params(("parallel", "parallel")),
        name=name,
    )(a, w, x)


def _ffn_up_kernel(h_ref, wa_ref, wb_ref, cw_ref, cb_ref, o_ref, abuf_ref, *, bm):
    i = pl.program_id(1)

    @pl.when(i % (SEQ // bm) == 0)
    def _():
        abuf_ref[0:8, :] = jnp.zeros((8, abuf_ref.shape[1]), F32)

    h = h_ref[...]
    a = jnp.dot(h, wa_ref[...], preferred_element_type=F32)
    abuf_ref[8:bm + 8, :] = a
    a1 = abuf_ref[7:bm + 7, :]
    a2 = abuf_ref[6:bm + 6, :]
    conv = a2 * cw_ref[0:1, :] + a1 * cw_ref[1:2, :] + a * cw_ref[2:3, :] + cb_ref[...]
    b = jnp.dot(h, wb_ref[...], preferred_element_type=F32)
    o_ref[...] = (jax.nn.gelu(conv) * b).astype(o_ref.dtype)
    abuf_ref[0:8, :] = abuf_ref[bm:bm + 8, :]


def _ffn_up(h, w_up, conv_w, conv_b, layer):
    bm, bn = 1024, 512
    n_j = D_FF // bn
    return pl.pallas_call(
        functools.partial(_ffn_up_kernel, bm=bm),
        grid=(n_j, TOKENS // bm),
        in_specs=[pl.BlockSpec((bm, D_MODEL), lambda j, i: (i, 0)),
                  pl.BlockSpec((None, D_MODEL, bn), lambda j, i: (layer, 0, j)),
                  pl.BlockSpec((None, D_MODEL, bn), lambda j, i: (layer, 0, n_j + j)),
                  pl.BlockSpec((None, 3, bn), lambda j, i: (layer, 0, j)),
                  pl.BlockSpec((None, 1, bn), lambda j, i: (layer, 0, j))],
        out_specs=pl.BlockSpec((bm, bn), lambda j, i: (i, j)),
        out_shape=jax.ShapeDtypeStruct((TOKENS, D_FF), BF16),
        scratch_shapes=[pltpu.VMEM((bm + 8, bn), F32)],
        compiler_params=_params(("parallel", "arbitrary")),
        name="ffn_up",
    )(h, w_up, w_up, conv_w, conv_b.reshape(DEPTH, 1, D_FF))


def kernel(x, rel_bias_table, norm_mix_g, w_in, conv_a_w, conv_a_b, w_branch_a, w_branch_b,
           w_o, norm_ffn_g, w_up, conv_f_w, conv_f_b, w_down, norm_final_g):
    w_in = w_in.astype(BF16)
    w_branch_a = w_branch_a.astype(BF16)
    w_branch_b = w_branch_b.astype(BF16)
    w_o = w_o.astype(BF16)
    w_up = w_up.astype(BF16)
    w_down = w_down.astype(BF16)
    bias = _bias_blocks(rel_bias_table)

    xs = x.reshape(TOKENS, D_MODEL)
    for l in range(DEPTH):
        h = _rmsnorm(xs, norm_mix_g[l], BF16)
        pm = _in_proj(h, w_in, l, 0, MAIN_COLS, False, "in_proj_main")
        gates = _in_proj(h, w_in, l, MAIN_COLS // 1024, GATE_COLS, True, "in_proj_gates")
        yb = _attention(pm, bias)
        merged = _mix(pm, yb, gates, w_branch_a, w_branch_b, conv_a_w, conv_a_b, l)
        xs = _resid_proj(merged, w_o, xs, l, 1024, 1024, False, "out_proj")
        h = _rmsnorm(xs, norm_ffn_g[l], BF16)
        g = _ffn_up(h, w_up, conv_f_w, conv_f_b, l)
        xs = _resid_proj(g, w_down, xs, l, 512, 1024, True, "ffn_down")
    out = _rmsnorm(xs, norm_final_g, F32)
    return out.reshape(BATCH, SEQ, D_MODEL)
```

```python
import functools
import math

import jax
import jax.numpy as jnp
import numpy as np
from jax import lax
from jax.experimental import pallas as pl
from jax.experimental.pallas import tpu as pltpu

D_MODEL = 4096
BATCH = 4
SEQ = 2048
DEPTH = 4
TOKENS = BATCH * SEQ

CONV_WIDTH = D_MODEL // 4
HEAD_DIM = 128
GROUPS = ((128, 1), (512, 4), (2048, 16))
HEADS_PER_GROUP = 8
N_ATTN_HEADS = HEADS_PER_GROUP * len(GROUPS)
ATTN_WIDTH = N_ATTN_HEADS * HEAD_DIM
MERGED_ATTN_WIDTH = HEADS_PER_GROUP * HEAD_DIM
BLOCK = 128
NUM_BUCKETS = 32
MAX_DISTANCE = 2048
D_FF = 2 * D_MODEL
EPS = 1e-6
NEG = -1e30

MAIN_COLS = 3 * CONV_WIDTH + 3 * ATTN_WIDTH
GATE_COLS = 2 * D_MODEL
Q_COL_BLOCK = 3 * CONV_WIDTH // HEAD_DIM
N_ATTN_BLOCKS = SEQ // BLOCK
assert all(window // dil == BLOCK for window, dil in GROUPS)

VMEM_LIMIT_BYTES = 56 * 1024 * 1024

BF16 = jnp.bfloat16
F32 = jnp.float32


def _params(semantics):
    return pltpu.CompilerParams(dimension_semantics=semantics,
                                vmem_limit_bytes=VMEM_LIMIT_BYTES)


def _rmsnorm_kernel(x_ref, g_ref, o_ref):
    x = x_ref[...]
    inv = lax.rsqrt(jnp.mean(x * x, axis=-1, keepdims=True) + EPS)
    o_ref[...] = (x * inv * g_ref[...]).astype(o_ref.dtype)


def _rmsnorm(x, g, out_dtype):
    bm = 512
    return pl.pallas_call(
        _rmsnorm_kernel,
        grid=(TOKENS // bm,),
        in_specs=[pl.BlockSpec((bm, D_MODEL), lambda i: (i, 0)),
                  pl.BlockSpec((1, D_MODEL), lambda i: (0, 0))],
        out_specs=pl.BlockSpec((bm, D_MODEL), lambda i: (i, 0)),
        out_shape=jax.ShapeDtypeStruct((TOKENS, D_MODEL), out_dtype),
        compiler_params=_params(("parallel",)),
        name="rmsnorm",
    )(x, g.reshape(1, D_MODEL))


def _stream_weights(w_refs, col_blocks, wbuf_ref, stage_ref, sem_ref, *, layer, row0, n_j, n_i):
    _, n_parts, k, bn = wbuf_ref.shape
    ck = k // n_i
    j = pl.program_id(0)
    t = j * n_i + pl.program_id(1)

    def copies(u, slot):
        block = (u // n_i) % n_j
        rows = pl.ds(row0 + (u % n_i) * ck, ck)
        return [pltpu.make_async_copy(w.at[layer, rows, pl.ds((cb + block) * bn, bn)],
                                      stage_ref.at[slot, p], sem_ref.at[slot, p])
                for p, (w, cb) in enumerate(zip(w_refs, col_blocks))]

    def cast(u, slot):
        rows = pl.ds(pl.multiple_of((u % n_i) * ck, ck), ck)
        for p in range(n_parts):
            wbuf_ref[(u // n_i) % 2, p, rows, :] = stage_ref[slot, p].astype(BF16)

    @pl.when(t == 0)
    def _():
        for cp in copies(0, 0):
            cp.start()
        for c in range(n_i - 1):
            for cp in copies(c + 1, (c + 1) % 2):
                cp.start()
            for cp in copies(c, c % 2):
                cp.wait()
            cast(c, c % 2)

    u = t + n_i - 1
    slot = u % 2
    for cp in copies(u, slot):
        cp.wait()

    @pl.when(t + 1 < n_j * n_i)
    def _():
        for cp in copies(u + 1, 1 - slot):
            cp.start()

    cast(u, slot)
    return wbuf_ref.at[j % 2]


def _stream_scratch(n_parts, k, bn, n_i):
    return [pltpu.VMEM((2, n_parts, k, bn), BF16),
            pltpu.VMEM((2, n_parts, k // n_i, bn), F32),
            pltpu.SemaphoreType.DMA((2, n_parts))]


_HBM = pl.BlockSpec(memory_space=pl.ANY)
_SEQUENTIAL = ("arbitrary", "arbitrary")
BM = 1024
BN = 1024
FFN_BN = 512
N_I = TOKENS // BM


def _proj_kernel(h_ref, w_hbm, o_ref, wbuf_ref, stage_ref, sem_ref, *, sigmoid, col_block0, **stream):
    w_ref = _stream_weights([w_hbm], [col_block0], wbuf_ref, stage_ref, sem_ref, row0=0, **stream)
    acc = jnp.dot(h_ref[...], w_ref[0], preferred_element_type=F32)
    if sigmoid:
        acc = jax.nn.sigmoid(acc)
    o_ref[...] = acc.astype(o_ref.dtype)


def _in_proj(h, w_in, layer, col_block0, n_cols, sigmoid, name):
    n_j = n_cols // BN
    return pl.pallas_call(
        functools.partial(_proj_kernel, sigmoid=sigmoid, layer=layer, col_block0=col_block0,
                          n_j=n_j, n_i=N_I),
        grid=(n_j, N_I),
        in_specs=[pl.BlockSpec((BM, D_MODEL), lambda j, i: (i, 0)), _HBM],
        out_specs=pl.BlockSpec((BM, BN), lambda j, i: (i, j)),
        out_shape=jax.ShapeDtypeStruct((TOKENS, n_cols), BF16),
        scratch_shapes=_stream_scratch(1, D_MODEL, BN, N_I),
        compiler_params=_params(_SEQUENTIAL),
        name=name,
    )(h, w_in)


def _attn_kernel(*refs):
    qkv_refs = refs[:9]
    bias_ref = refs[9]
    o_ref = refs[10]
    stage_ref, stage2_ref, sub_ref, p_ref, on_ref, ln_ref = refs[11:]
    scale = HEAD_DIM ** -0.5

    row = lax.broadcasted_iota(jnp.int32, (BLOCK, BLOCK), 0)
    lane = lax.broadcasted_iota(jnp.int32, (BLOCK, BLOCK), 1)
    valid_cur = lane <= row
    valid_band = jnp.concatenate([lane >= row, valid_cur], axis=1)

    for g, (_, dil) in enumerate(GROUPS):
        sub_len = SEQ // dil
        blocks_per_sub = sub_len // BLOCK

        def to_sub(part):
            stage = stage_ref.at[g - 1, part]
            dst = sub_ref.at[g - 1, part]
            stage[...] = qkv_refs[3 * g + part][...].astype(F32)
            if dil == 4:
                for r in range(4):
                    rows = stage[pl.ds(r, sub_len, stride=4), :]
                    dst[r * sub_len:(r + 1) * sub_len, :] = rows.astype(BF16)
            else:
                stage2 = stage2_ref.at[part]
                quarter = SEQ // 4
                for b in range(4):
                    stage2[b * quarter:(b + 1) * quarter, :] = stage[pl.ds(b, quarter, stride=4), :]
                for b in range(4):
                    for a in range(4):
                        rows = stage2[pl.ds(b * quarter + a, sub_len, stride=4), :]
                        r = 4 * a + b
                        dst[r * sub_len:(r + 1) * sub_len, :] = rows.astype(BF16)
            return dst

        if dil == 1:
            qs_ref, ks_ref, vs_ref = qkv_refs[0:3]
        else:
            qs_ref, ks_ref, vs_ref = to_sub(0), to_sub(1), to_sub(2)

        for blk in range(N_ATTN_BLOCKS):
            r, n = divmod(blk, blocks_per_sub)
            row0 = blk * BLOCK
            qb = qs_ref[row0:row0 + BLOCK, :]
            if n > 0:
                kb = ks_ref[row0 - BLOCK:row0 + BLOCK, :]
                s = lax.dot_general(qb, kb, (((1,), (1,)), ((), ())),
                                    preferred_element_type=F32) * scale
                s = jnp.where(valid_band, s + bias_ref[g], NEG)
                m = jnp.max(s, axis=-1, keepdims=True)
                p = jnp.exp(s - m)
                den = jnp.sum(p, axis=-1, keepdims=True)
            else:
                kb = ks_ref[row0:row0 + BLOCK, :]
                s = lax.dot_general(qb, kb, (((1,), (1,)), ((), ())),
                                    preferred_element_type=F32) * scale
                s = jnp.where(valid_cur, s + bias_ref[g, :, BLOCK:], NEG)
                m = jnp.maximum(jnp.max(s, axis=-1, keepdims=True), NEG)
                p = jnp.exp(s - m)
                den = jnp.sum(p, axis=-1, keepdims=True) + BLOCK * jnp.exp(NEG - m)
            p_ref[g, row0:row0 + BLOCK, 0:p.shape[1]] = (p / den).astype(BF16)
            lse = jnp.broadcast_to(m + jnp.log(den), (BLOCK, HEAD_DIM))
            if dil == 1:
                ln_ref[g, row0:row0 + BLOCK, :] = lse
            else:
                ln_ref[g, pl.ds(n * BLOCK * dil + r, BLOCK, stride=dil), :] = lse

        for blk in range(N_ATTN_BLOCKS):
            r, n = divmod(blk, blocks_per_sub)
            row0 = blk * BLOCK
            if n > 0:
                o = jnp.dot(p_ref[g, row0:row0 + BLOCK, :], vs_ref[row0 - BLOCK:row0 + BLOCK, :],
                            preferred_element_type=F32)
            else:
                o = jnp.dot(p_ref[g, row0:row0 + BLOCK, 0:BLOCK], vs_ref[row0:row0 + BLOCK, :],
                            preferred_element_type=F32)
            if dil == 1:
                on_ref[g, row0:row0 + BLOCK, :] = o
            else:
                on_ref[g, pl.ds(n * BLOCK * dil + r, BLOCK, stride=dil), :] = o

    l0, l1, l2 = ln_ref[0], ln_ref[1], ln_ref[2]
    mx = jnp.maximum(jnp.maximum(l0, l1), l2)
    e0, e1, e2 = jnp.exp(l0 - mx), jnp.exp(l1 - mx), jnp.exp(l2 - mx)
    tot = e0 + e1 + e2
    out = (e0 / tot) * on_ref[0] + (e1 / tot) * on_ref[1] + (e2 / tot) * on_ref[2]
    o_ref[...] = out.astype(o_ref.dtype)


def _attention(pm, bias):
    in_specs = []
    for g in range(len(GROUPS)):
        for part in range(3):
            col0 = Q_COL_BLOCK + part * N_ATTN_HEADS + g * HEADS_PER_GROUP
            in_specs.append(pl.BlockSpec(
                (SEQ, HEAD_DIM), lambda b, h, col0=col0: (b, col0 + h)))
    in_specs.append(pl.BlockSpec((len(GROUPS), None, BLOCK, 2 * BLOCK),
                                 lambda b, h: (0, h, 0, 0)))
    return pl.pallas_call(
        _attn_kernel,
        grid=(BATCH, HEADS_PER_GROUP),
        in_specs=in_specs,
        out_specs=pl.BlockSpec((SEQ, HEAD_DIM), lambda b, h: (b, h)),
        out_shape=jax.ShapeDtypeStruct((TOKENS, MERGED_ATTN_WIDTH), BF16),
        scratch_shapes=[
            pltpu.VMEM((2, 3, SEQ, HEAD_DIM), F32),
            pltpu.VMEM((3, SEQ, HEAD_DIM), F32),
            pltpu.VMEM((2, 3, SEQ, HEAD_DIM), BF16),
            pltpu.VMEM((len(GROUPS), SEQ, 2 * BLOCK), BF16),
            pltpu.VMEM((len(GROUPS), SEQ, HEAD_DIM), F32),
            pltpu.VMEM((len(GROUPS), SEQ, HEAD_DIM), F32),
        ],
        compiler_params=_params(("parallel", "parallel")),
        name="dilated_attention",
    )(*([pm] * 9), bias)


def _t5_bucket(dist):
    max_exact = NUM_BUCKETS // 2
    distf = jnp.maximum(dist, 1).astype(F32)
    large = max_exact + (jnp.log(distf / max_exact) / math.log(MAX_DISTANCE / max_exact)
                         * (NUM_BUCKETS - max_exact)).astype(jnp.int32)
    large = jnp.minimum(large, NUM_BUCKETS - 1)
    return jnp.where(dist < max_exact, dist, large)


def _bias_blocks(rel_bias_table):
    q_idx = jnp.arange(BLOCK)
    k_idx = jnp.arange(2 * BLOCK)
    delta = (q_idx[:, None] + BLOCK) - k_idx[None, :]
    out = []
    for g, (window, dil) in enumerate(GROUPS):
        steps = window // dil
        table_g = rel_bias_table[:, g * HEADS_PER_GROUP:(g + 1) * HEADS_PER_GROUP]
        bias_delta = table_g[_t5_bucket(jnp.arange(steps + 1) * dil)].astype(F32)
        onehot = (jnp.clip(delta, 0, steps)[:, :, None] == jnp.arange(steps + 1)).astype(F32)
        out.append(jnp.einsum('qkj,jh->hqk', onehot, bias_delta, precision=lax.Precision.HIGHEST))
    return jnp.stack(out, 0)


def _conv_mixer_kernel(ah_ref, ab_ref, ac_ref, ahp_ref, acp_ref, cw_ref, cb_ref, o_ref, ubuf_ref, *, bm):
    i = pl.program_id(0)
    u = ac_ref[...].astype(F32) * ah_ref[...].astype(F32)
    halo = acp_ref[...].astype(F32) * ahp_ref[...].astype(F32)
    halo = jnp.where(i % (SEQ // bm) == 0, 0.0, halo)
    ubuf_ref[0:8, :] = halo[8:16, :]
    ubuf_ref[8:bm + 8, :] = u
    u1 = ubuf_ref[7:bm + 7, :]
    u2 = ubuf_ref[6:bm + 6, :]
    conv = u2 * cw_ref[0:1, :] + u1 * cw_ref[1:2, :] + u * cw_ref[2:3, :] + cb_ref[...]
    o_ref[...] = (ab_ref[...].astype(F32) * conv).astype(o_ref.dtype)


def _conv_mixer(pm, conv_w, conv_b, layer):
    bm = 512
    halo = 16

    def prev_rows(col):
        return lambda i: (jnp.maximum(i * (bm // halo) - 1, 0), col)

    return pl.pallas_call(
        functools.partial(_conv_mixer_kernel, bm=bm),
        grid=(TOKENS // bm,),
        in_specs=[
            pl.BlockSpec((bm, CONV_WIDTH), lambda i: (i, 0)),
            pl.BlockSpec((bm, CONV_WIDTH), lambda i: (i, 1)),
            pl.BlockSpec((bm, CONV_WIDTH), lambda i: (i, 2)),
            pl.BlockSpec((halo, CONV_WIDTH), prev_rows(0)),
            pl.BlockSpec((halo, CONV_WIDTH), prev_rows(2)),
            pl.BlockSpec((None, 3, CONV_WIDTH), lambda i: (layer, 0, 0)),
            pl.BlockSpec((None, 1, CONV_WIDTH), lambda i: (layer, 0, 0)),
        ],
        out_specs=pl.BlockSpec((bm, CONV_WIDTH), lambda i: (i, 0)),
        out_shape=jax.ShapeDtypeStruct((TOKENS, CONV_WIDTH), BF16),
        scratch_shapes=[pltpu.VMEM((bm + 8, CONV_WIDTH), F32)],
        compiler_params=_params(("parallel",)),
        name="conv_mixer",
    )(pm, pm, pm, pm, pm, conv_w, conv_b.reshape(DEPTH, 1, CONV_WIDTH))


def _mix_kernel(ya_ref, yb_ref, sa_ref, sb_ref, wa_hbm, wb_hbm, o_ref, wbuf_ref, stage_ref, sem_ref,
                **stream):
    w_ref = _stream_weights([wa_hbm, wb_hbm], [0, 0], wbuf_ref, stage_ref, sem_ref, row0=0, **stream)
    br_a = jnp.dot(ya_ref[...], w_ref[0], preferred_element_type=F32)
    br_b = jnp.dot(yb_ref[...], w_ref[1], preferred_element_type=F32)
    merged = sa_ref[...].astype(F32) * br_a + sb_ref[...].astype(F32) * br_b
    o_ref[...] = merged.astype(o_ref.dtype)


def _mix(ya, yb, gates, w_a, w_b, layer):
    n_j = D_MODEL // BN
    return pl.pallas_call(
        functools.partial(_mix_kernel, layer=layer, n_j=n_j, n_i=N_I),
        grid=(n_j, N_I),
        in_specs=[
            pl.BlockSpec((BM, CONV_WIDTH), lambda j, i: (i, 0)),
            pl.BlockSpec((BM, MERGED_ATTN_WIDTH), lambda j, i: (i, 0)),
            pl.BlockSpec((BM, BN), lambda j, i: (i, j)),
            pl.BlockSpec((BM, BN), lambda j, i: (i, n_j + j)),
            _HBM, _HBM,
        ],
        out_specs=pl.BlockSpec((BM, BN), lambda j, i: (i, j)),
        out_shape=jax.ShapeDtypeStruct((TOKENS, D_MODEL), BF16),
        scratch_shapes=_stream_scratch(2, CONV_WIDTH, BN, N_I),
        compiler_params=_params(_SEQUENTIAL),
        name="mix_branches",
    )(ya, yb, gates, gates, w_a, w_b)


def _resid_kernel(a_ref, x_ref, w_hbm, o_ref, wbuf_ref, stage_ref, sem_ref, **stream):
    w_ref = _stream_weights([w_hbm], [0], wbuf_ref, stage_ref, sem_ref, **stream)
    o_ref[...] = x_ref[...] + jnp.dot(a_ref[...], w_ref[0], preferred_element_type=F32)


def _resid_proj(a, w, x, layer, k_block, name):
    bm = BM // 2
    n_j, n_i = D_MODEL // BN, TOKENS // bm
    return pl.pallas_call(
        functools.partial(_resid_kernel, layer=layer, row0=k_block * D_MODEL, n_j=n_j, n_i=n_i),
        grid=(n_j, n_i),
        in_specs=[pl.BlockSpec((bm, D_MODEL), lambda j, i: (i, k_block)),
                  pl.BlockSpec((bm, BN), lambda j, i: (i, j)),
                  _HBM],
        out_specs=pl.BlockSpec((bm, BN), lambda j, i: (i, j)),
        out_shape=jax.ShapeDtypeStruct((TOKENS, D_MODEL), F32),
        scratch_shapes=_stream_scratch(1, D_MODEL, BN, n_i),
        compiler_params=_params(_SEQUENTIAL),
        name=name,
    )(a, x, w)


def _ffn_up_kernel(h_ref, cw_ref, cb_ref, w_hbm, o_ref, wbuf_ref, stage_ref, sem_ref, abuf_ref,
                   *, n_j, **stream):
    w_ref = _stream_weights([w_hbm, w_hbm], [0, n_j], wbuf_ref, stage_ref, sem_ref,
                            row0=0, n_j=n_j, **stream)
    i = pl.program_id(1)

    @pl.when(i % (SEQ // BM) == 0)
    def _():
        abuf_ref[0:8, :] = jnp.zeros((8, abuf_ref.shape[1]), F32)

    h = h_ref[...]
    a = jnp.dot(h, w_ref[0], preferred_element_type=F32)
    abuf_ref[8:BM + 8, :] = a
    a1 = abuf_ref[7:BM + 7, :]
    a2 = abuf_ref[6:BM + 6, :]
    conv = a2 * cw_ref[0:1, :] + a1 * cw_ref[1:2, :] + a * cw_ref[2:3, :] + cb_ref[...]
    b = jnp.dot(h, w_ref[1], preferred_element_type=F32)
    o_ref[...] = (jax.nn.gelu(conv) * b).astype(o_ref.dtype)
    abuf_ref[0:8, :] = abuf_ref[BM:BM + 8, :]


def _ffn_up(h, w_up, conv_w, conv_b, layer):
    n_j = D_FF // FFN_BN
    return pl.pallas_call(
        functools.partial(_ffn_up_kernel, layer=layer, n_j=n_j, n_i=N_I),
        grid=(n_j, N_I),
        in_specs=[pl.BlockSpec((BM, D_MODEL), lambda j, i: (i, 0)),
                  pl.BlockSpec((None, 3, FFN_BN), lambda j, i: (layer, 0, j)),
                  pl.BlockSpec((None, 1, FFN_BN), lambda j, i: (layer, 0, j)),
                  _HBM],
        out_specs=pl.BlockSpec((BM, FFN_BN), lambda j, i: (i, j)),
        out_shape=jax.ShapeDtypeStruct((TOKENS, D_FF), BF16),
        scratch_shapes=_stream_scratch(2, D_MODEL, FFN_BN, N_I) + [pltpu.VMEM((BM + 8, FFN_BN), F32)],
        compiler_params=_params(_SEQUENTIAL),
        name="ffn_up",
    )(h, conv_w, conv_b.reshape(DEPTH, 1, D_FF), w_up)


def kernel(x, rel_bias_table, norm_mix_g, w_in, conv_a_w, conv_a_b, w_branch_a, w_branch_b,
           w_o, norm_ffn_g, w_up, conv_f_w, conv_f_b, w_down, norm_final_g):
    bias = _bias_blocks(rel_bias_table)
    xs = x.reshape(TOKENS, D_MODEL)
    for l in range(DEPTH):
        h = _rmsnorm(xs, norm_mix_g[l], BF16)
        pm = _in_proj(h, w_in, l, 0, MAIN_COLS, False, "in_proj_main")
        gates = _in_proj(h, w_in, l, MAIN_COLS // BN, GATE_COLS, True, "in_proj_gates")
        yb = _attention(pm, bias)
        ya = _conv_mixer(pm, conv_a_w, conv_a_b, l)
        merged = _mix(ya, yb, gates, w_branch_a, w_branch_b, l)
        xs = _resid_proj(merged, w_o, xs, l, 0, "out_proj")
        h = _rmsnorm(xs, norm_ffn_g[l], BF16)
        g = _ffn_up(h, w_up, conv_f_w, conv_f_b, l)
        for k_block in range(D_FF // D_MODEL):
            xs = _resid_proj(g, w_down, xs, l, k_block, "ffn_down")
    out = _rmsnorm(xs, norm_final_g, F32)
    return out.reshape(BATCH, SEQ, D_MODEL)
```

```python
import functools
import math

import jax
import jax.numpy as jnp
import numpy as np
from jax import lax
from jax.experimental import pallas as pl
from jax.experimental.pallas import tpu as pltpu

D_MODEL = 4096
BATCH = 4
SEQ = 2048
DEPTH = 4
TOKENS = BATCH * SEQ

CONV_WIDTH = D_MODEL // 4
HEAD_DIM = 128
GROUPS = ((128, 1), (512, 4), (2048, 16))
HEADS_PER_GROUP = 8
N_ATTN_HEADS = HEADS_PER_GROUP * len(GROUPS)
ATTN_WIDTH = N_ATTN_HEADS * HEAD_DIM
MERGED_ATTN_WIDTH = HEADS_PER_GROUP * HEAD_DIM
BLOCK = 128
NUM_BUCKETS = 32
MAX_DISTANCE = 2048
D_FF = 2 * D_MODEL
EPS = 1e-6
NEG = -1e30

MAIN_COLS = 3 * CONV_WIDTH + 3 * ATTN_WIDTH
GATE_COLS = 2 * D_MODEL
Q_COL_BLOCK = 3 * CONV_WIDTH // HEAD_DIM
N_ATTN_BLOCKS = SEQ // BLOCK
assert all(window // dil == BLOCK for window, dil in GROUPS)

VMEM_LIMIT_BYTES = 60000 * 1024

BF16 = jnp.bfloat16
F32 = jnp.float32


def _params(semantics):
    return pltpu.CompilerParams(dimension_semantics=semantics,
                                vmem_limit_bytes=VMEM_LIMIT_BYTES)


def _rmsnorm_kernel(x_ref, g_ref, o_ref):
    x = x_ref[...]
    inv = lax.rsqrt(jnp.mean(x * x, axis=-1, keepdims=True) + EPS)
    o_ref[...] = (x * inv * g_ref[...]).astype(o_ref.dtype)


def _rmsnorm(x, g, out_dtype):
    bm = 512
    return pl.pallas_call(
        _rmsnorm_kernel,
        grid=(TOKENS // bm,),
        in_specs=[pl.BlockSpec((bm, D_MODEL), lambda i: (i, 0)),
                  pl.BlockSpec((1, D_MODEL), lambda i: (0, 0))],
        out_specs=pl.BlockSpec((bm, D_MODEL), lambda i: (i, 0)),
        out_shape=jax.ShapeDtypeStruct((TOKENS, D_MODEL), out_dtype),
        compiler_params=_params(("parallel",)),
        name="rmsnorm",
    )(x, g.reshape(1, D_MODEL))


def _row_stats_kernel(x_ref, xb_ref, ssq_ref):
    x = x_ref[...]
    xb_ref[...] = x.astype(BF16)
    ssq_ref[...] = _lane_partial_sumsq(x)


def _row_stats(x):
    bm = 512
    return pl.pallas_call(
        _row_stats_kernel,
        grid=(TOKENS // bm,),
        in_specs=[pl.BlockSpec((bm, D_MODEL), lambda i: (i, 0))],
        out_specs=[pl.BlockSpec((bm, D_MODEL), lambda i: (i, 0)),
                   pl.BlockSpec((None, bm, LANES), lambda i: (0, i, 0))],
        out_shape=[jax.ShapeDtypeStruct((TOKENS, D_MODEL), BF16),
                   jax.ShapeDtypeStruct((1, TOKENS, LANES), F32)],
        compiler_params=_params(("parallel",)),
        name="row_stats",
    )(x)


def _stream_weights(w_refs, col_blocks, wbuf_ref, stage_ref, sem_ref, *, layer, row0, n_j, n_i,
                    row_gain_ref=None):
    _, n_parts, k, bn = wbuf_ref.shape
    ck = k // n_i
    j = pl.program_id(0)
    t = j * n_i + pl.program_id(1)

    def copies(u, slot):
        block = (u // n_i) % n_j
        rows = pl.ds(row0 + (u % n_i) * ck, ck)
        return [pltpu.make_async_copy(w.at[layer, rows, pl.ds((cb + block) * bn, bn)],
                                      stage_ref.at[slot, p], sem_ref.at[slot, p])
                for p, (w, cb) in enumerate(zip(w_refs, col_blocks))]

    def cast(u, slot):
        rows = pl.ds(pl.multiple_of((u % n_i) * ck, ck), ck)
        for p in range(n_parts):
            w = stage_ref[slot, p]
            if row_gain_ref is not None:
                w = w * jnp.concatenate([row_gain_ref[rows, :]] * (bn // LANES), axis=1)
            wbuf_ref[(u // n_i) % 2, p, rows, :] = w.astype(BF16)

    @pl.when(t == 0)
    def _():
        for cp in copies(0, 0):
            cp.start()
        for c in range(n_i - 1):
            for cp in copies(c + 1, (c + 1) % 2):
                cp.start()
            for cp in copies(c, c % 2):
                cp.wait()
            cast(c, c % 2)

    u = t + n_i - 1
    slot = u % 2
    for cp in copies(u, slot):
        cp.wait()

    @pl.when(t + 1 < n_j * n_i)
    def _():
        for cp in copies(u + 1, 1 - slot):
            cp.start()

    cast(u, slot)
    return wbuf_ref.at[j % 2]


def _stream_scratch(n_parts, k, bn, n_i):
    return [pltpu.VMEM((2, n_parts, k, bn), BF16),
            pltpu.VMEM((2, n_parts, k // n_i, bn), F32),
            pltpu.SemaphoreType.DMA((2, n_parts))]


def _row_tiles(n_rows, tile):
    return [slice(r, r + tile) for r in range(0, n_rows, tile)]


def _row_inv_rms(ssq_ref, rows):
    tot = ssq_ref[0, rows, :]
    for p in range(1, ssq_ref.shape[0]):
        tot = tot + ssq_ref[p, rows, :]
    return lax.rsqrt(jnp.sum(tot, axis=-1, keepdims=True) * (1.0 / D_MODEL) + EPS)


def _lane_partial_sumsq(x):
    sq = x * x
    tot = sq[:, 0:LANES]
    for c in range(1, x.shape[1] // LANES):
        tot = tot + sq[:, c * LANES:(c + 1) * LANES]
    return tot


_HBM = pl.BlockSpec(memory_space=pl.ANY)
_SEQUENTIAL = ("arbitrary", "arbitrary")
BM = 1024
BN = 1024
FFN_BN = 512
N_I = TOKENS // BM
ROW_TILE = 256
LANES = 128


def _proj_kernel(xb_ref, ssq_ref, gain_ref, w_hbm, o_ref, wbuf_ref, stage_ref, sem_ref,
                 *, sigmoid, col_block0, **stream):
    w_ref = _stream_weights([w_hbm], [col_block0], wbuf_ref, stage_ref, sem_ref, row0=0,
                            row_gain_ref=gain_ref, **stream)
    for rows in _row_tiles(BM, ROW_TILE):
        acc = jnp.dot(xb_ref[rows, :], w_ref[0], preferred_element_type=F32)
        acc = acc * _row_inv_rms(ssq_ref, rows)
        if sigmoid:
            acc = jax.nn.sigmoid(acc)
        o_ref[rows, :] = acc.astype(o_ref.dtype)


def _norm_in_specs(n_ssq_parts, layer):
    return [pl.BlockSpec((BM, D_MODEL), lambda j, i: (i, 0)),
            pl.BlockSpec((n_ssq_parts, BM, LANES), lambda j, i: (0, i, 0)),
            pl.BlockSpec((None, D_MODEL, LANES), lambda j, i: (layer, 0, 0),
                         pipeline_mode=pl.Buffered(1))]


def _in_proj(xb, ssq, gain, w_in, layer, col_block0, n_cols, sigmoid, name):
    n_j = n_cols // BN
    return pl.pallas_call(
        functools.partial(_proj_kernel, sigmoid=sigmoid, layer=layer, col_block0=col_block0,
                          n_j=n_j, n_i=N_I),
        grid=(n_j, N_I),
        in_specs=_norm_in_specs(ssq.shape[0], layer) + [_HBM],
        out_specs=pl.BlockSpec((BM, BN), lambda j, i: (i, j)),
        out_shape=jax.ShapeDtypeStruct((TOKENS, n_cols), BF16),
        scratch_shapes=_stream_scratch(1, D_MODEL, BN, N_I),
        compiler_params=_params(_SEQUENTIAL),
        name=name,
    )(xb, ssq, gain, w_in)


def _attn_kernel(*refs):
    qkv_refs = refs[:9]
    bias_ref = refs[9]
    o_ref = refs[10]
    stage_ref, stage2_ref, sub_ref, p_ref, on_ref, ln_ref = refs[11:]
    scale = HEAD_DIM ** -0.5

    row = lax.broadcasted_iota(jnp.int32, (BLOCK, BLOCK), 0)
    lane = lax.broadcasted_iota(jnp.int32, (BLOCK, BLOCK), 1)
    valid_cur = lane <= row
    valid_band = jnp.concatenate([lane >= row, valid_cur], axis=1)

    for g, (_, dil) in enumerate(GROUPS):
        sub_len = SEQ // dil
        blocks_per_sub = sub_len // BLOCK

        def to_sub(part):
            stage = stage_ref.at[g - 1, part]
            dst = sub_ref.at[g - 1, part]
            stage[...] = qkv_refs[3 * g + part][...].astype(F32)
            if dil == 4:
                for r in range(4):
                    rows = stage[pl.ds(r, sub_len, stride=4), :]
                    dst[r * sub_len:(r + 1) * sub_len, :] = rows.astype(BF16)
            else:
                stage2 = stage2_ref.at[part]
                quarter = SEQ // 4
                for b in range(4):
                    stage2[b * quarter:(b + 1) * quarter, :] = stage[pl.ds(b, quarter, stride=4), :]
                for b in range(4):
                    for a in range(4):
                        rows = stage2[pl.ds(b * quarter + a, sub_len, stride=4), :]
                        r = 4 * a + b
                        dst[r * sub_len:(r + 1) * sub_len, :] = rows.astype(BF16)
            return dst

        if dil == 1:
            qs_ref, ks_ref, vs_ref = qkv_refs[0:3]
        else:
            qs_ref, ks_ref, vs_ref = to_sub(0), to_sub(1), to_sub(2)

        for blk in range(N_ATTN_BLOCKS):
            r, n = divmod(blk, blocks_per_sub)
            row0 = blk * BLOCK
            qb = qs_ref[row0:row0 + BLOCK, :]
            if n > 0:
                kb = ks_ref[row0 - BLOCK:row0 + BLOCK, :]
                s = lax.dot_general(qb, kb, (((1,), (1,)), ((), ())),
                                    preferred_element_type=F32) * scale
                s = jnp.where(valid_band, s + bias_ref[g], NEG)
                m = jnp.max(s, axis=-1, keepdims=True)
                p = jnp.exp(s - m)
                den = jnp.sum(p, axis=-1, keepdims=True)
            else:
                kb = ks_ref[row0:row0 + BLOCK, :]
                s = lax.dot_general(qb, kb, (((1,), (1,)), ((), ())),
                                    preferred_element_type=F32) * scale
                s = jnp.where(valid_cur, s + bias_ref[g, :, BLOCK:], NEG)
                m = jnp.maximum(jnp.max(s, axis=-1, keepdims=True), NEG)
                p = jnp.exp(s - m)
                den = jnp.sum(p, axis=-1, keepdims=True) + BLOCK * jnp.exp(NEG - m)
            p_ref[g, row0:row0 + BLOCK, 0:p.shape[1]] = (p / den).astype(BF16)
            lse = jnp.broadcast_to(m + jnp.log(den), (BLOCK, HEAD_DIM))
            if dil == 1:
                ln_ref[g, row0:row0 + BLOCK, :] = lse
            else:
                ln_ref[g, pl.ds(n * BLOCK * dil + r, BLOCK, stride=dil), :] = lse

        for blk in range(N_ATTN_BLOCKS):
            r, n = divmod(blk, blocks_per_sub)
            row0 = blk * BLOCK
            if n > 0:
                o = jnp.dot(p_ref[g, row0:row0 + BLOCK, :], vs_ref[row0 - BLOCK:row0 + BLOCK, :],
                            preferred_element_type=F32)
            else:
                o = jnp.dot(p_ref[g, row0:row0 + BLOCK, 0:BLOCK], vs_ref[row0:row0 + BLOCK, :],
                            preferred_element_type=F32)
            if dil == 1:
                on_ref[g, row0:row0 + BLOCK, :] = o
            else:
                on_ref[g, pl.ds(n * BLOCK * dil + r, BLOCK, stride=dil), :] = o

    l0, l1, l2 = ln_ref[0], ln_ref[1], ln_ref[2]
    mx = jnp.maximum(jnp.maximum(l0, l1), l2)
    e0, e1, e2 = jnp.exp(l0 - mx), jnp.exp(l1 - mx), jnp.exp(l2 - mx)
    tot = e0 + e1 + e2
    out = (e0 / tot) * on_ref[0] + (e1 / tot) * on_ref[1] + (e2 / tot) * on_ref[2]
    o_ref[...] = out.astype(o_ref.dtype)


def _attention(pm, bias):
    in_specs = []
    for g in range(len(GROUPS)):
        for part in range(3):
            col0 = Q_COL_BLOCK + part * N_ATTN_HEADS + g * HEADS_PER_GROUP
            in_specs.append(pl.BlockSpec(
                (SEQ, HEAD_DIM), lambda b, h, col0=col0: (b, col0 + h)))
    in_specs.append(pl.BlockSpec((len(GROUPS), None, BLOCK, 2 * BLOCK),
                                 lambda b, h: (0, h, 0, 0)))
    return pl.pallas_call(
        _attn_kernel,
        grid=(BATCH, HEADS_PER_GROUP),
        in_specs=in_specs,
        out_specs=pl.BlockSpec((SEQ, HEAD_DIM), lambda b, h: (b, h)),
        out_shape=jax.ShapeDtypeStruct((TOKENS, MERGED_ATTN_WIDTH), BF16),
        scratch_shapes=[
            pltpu.VMEM((2, 3, SEQ, HEAD_DIM), F32),
            pltpu.VMEM((3, SEQ, HEAD_DIM), F32),
            pltpu.VMEM((2, 3, SEQ, HEAD_DIM), BF16),
            pltpu.VMEM((len(GROUPS), SEQ, 2 * BLOCK), BF16),
            pltpu.VMEM((len(GROUPS), SEQ, HEAD_DIM), F32),
            pltpu.VMEM((len(GROUPS), SEQ, HEAD_DIM), F32),
        ],
        compiler_params=_params(("parallel", "parallel")),
        name="dilated_attention",
    )(*([pm] * 9), bias)


def _t5_bucket(dist):
    max_exact = NUM_BUCKETS // 2
    distf = jnp.maximum(dist, 1).astype(F32)
    large = max_exact + (jnp.log(distf / max_exact) / math.log(MAX_DISTANCE / max_exact)
                         * (NUM_BUCKETS - max_exact)).astype(jnp.int32)
    large = jnp.minimum(large, NUM_BUCKETS - 1)
    return jnp.where(dist < max_exact, dist, large)


def _bias_blocks(rel_bias_table):
    q_idx = jnp.arange(BLOCK)
    k_idx = jnp.arange(2 * BLOCK)
    delta = (q_idx[:, None] + BLOCK) - k_idx[None, :]
    out = []
    for g, (window, dil) in enumerate(GROUPS):
        steps = window // dil
        table_g = rel_bias_table[:, g * HEADS_PER_GROUP:(g + 1) * HEADS_PER_GROUP]
        bias_delta = table_g[_t5_bucket(jnp.arange(steps + 1) * dil)].astype(F32)
        onehot = (jnp.clip(delta, 0, steps)[:, :, None] == jnp.arange(steps + 1)).astype(F32)
        out.append(jnp.einsum('qkj,jh->hqk', onehot, bias_delta, precision=lax.Precision.HIGHEST))
    return jnp.stack(out, 0)


def _conv_mixer_kernel(ah_ref, ab_ref, ac_ref, ahp_ref, acp_ref, cw_ref, cb_ref, o_ref, ubuf_ref, *, bm):
    i = pl.program_id(0)
    u = ac_ref[...].astype(F32) * ah_ref[...].astype(F32)
    halo = acp_ref[...].astype(F32) * ahp_ref[...].astype(F32)
    halo = jnp.where(i % (SEQ // bm) == 0, 0.0, halo)
    ubuf_ref[0:8, :] = halo[8:16, :]
    ubuf_ref[8:bm + 8, :] = u
    u1 = ubuf_ref[7:bm + 7, :]
    u2 = ubuf_ref[6:bm + 6, :]
    conv = u2 * cw_ref[0:1, :] + u1 * cw_ref[1:2, :] + u * cw_ref[2:3, :] + cb_ref[...]
    o_ref[...] = (ab_ref[...].astype(F32) * conv).astype(o_ref.dtype)


def _conv_mixer(pm, conv_w, conv_b, layer):
    bm = 512
    halo = 16

    def prev_rows(col):
        return lambda i: (jnp.maximum(i * (bm // halo) - 1, 0), col)

    return pl.pallas_call(
        functools.partial(_conv_mixer_kernel, bm=bm),
        grid=(TOKENS // bm,),
        in_specs=[
            pl.BlockSpec((bm, CONV_WIDTH), lambda i: (i, 0)),
            pl.BlockSpec((bm, CONV_WIDTH), lambda i: (i, 1)),
            pl.BlockSpec((bm, CONV_WIDTH), lambda i: (i, 2)),
            pl.BlockSpec((halo, CONV_WIDTH), prev_rows(0)),
            pl.BlockSpec((halo, CONV_WIDTH), prev_rows(2)),
            pl.BlockSpec((None, 3, CONV_WIDTH), lambda i: (layer, 0, 0)),
            pl.BlockSpec((None, 1, CONV_WIDTH), lambda i: (layer, 0, 0)),
        ],
        out_specs=pl.BlockSpec((bm, CONV_WIDTH), lambda i: (i, 0)),
        out_shape=jax.ShapeDtypeStruct((TOKENS, CONV_WIDTH), BF16),
        scratch_shapes=[pltpu.VMEM((bm + 8, CONV_WIDTH), F32)],
        compiler_params=_params(("parallel",)),
        name="conv_mixer",
    )(pm, pm, pm, pm, pm, conv_w, conv_b.reshape(DEPTH, 1, CONV_WIDTH))


def _mix_kernel(ya_ref, yb_ref, sa_ref, sb_ref, wa_hbm, wb_hbm, o_ref, wbuf_ref, stage_ref, sem_ref,
                **stream):
    w_ref = _stream_weights([wa_hbm, wb_hbm], [0, 0], wbuf_ref, stage_ref, sem_ref, row0=0, **stream)
    for rows in _row_tiles(BM, ROW_TILE):
        br_a = jnp.dot(ya_ref[rows, :], w_ref[0], preferred_element_type=F32)
        br_b = jnp.dot(yb_ref[rows, :], w_ref[1], preferred_element_type=F32)
        merged = sa_ref[rows, :].astype(F32) * br_a + sb_ref[rows, :].astype(F32) * br_b
        o_ref[rows, :] = merged.astype(o_ref.dtype)


def _mix(ya, yb, gates, w_a, w_b, layer):
    n_j = D_MODEL // BN
    return pl.pallas_call(
        functools.partial(_mix_kernel, layer=layer, n_j=n_j, n_i=N_I),
        grid=(n_j, N_I),
        in_specs=[
            pl.BlockSpec((BM, CONV_WIDTH), lambda j, i: (i, 0)),
            pl.BlockSpec((BM, MERGED_ATTN_WIDTH), lambda j, i: (i, 0)),
            pl.BlockSpec((BM, BN), lambda j, i: (i, j)),
            pl.BlockSpec((BM, BN), lambda j, i: (i, n_j + j)),
            _HBM, _HBM,
        ],
        out_specs=pl.BlockSpec((BM, BN), lambda j, i: (i, j)),
        out_shape=jax.ShapeDtypeStruct((TOKENS, D_MODEL), BF16),
        scratch_shapes=_stream_scratch(2, CONV_WIDTH, BN, N_I),
        compiler_params=_params(_SEQUENTIAL),
        name="mix_branches",
    )(ya, yb, gates, gates, w_a, w_b)


def _resid_kernel(a_ref, x_ref, w_hbm, o_ref, *rest, emit_stats, **stream):
    if emit_stats:
        xb_ref, ssq_ref, wbuf_ref, stage_ref, sem_ref = rest
    else:
        wbuf_ref, stage_ref, sem_ref = rest
    w_ref = _stream_weights([w_hbm], [0], wbuf_ref, stage_ref, sem_ref, **stream)
    for rows in _row_tiles(a_ref.shape[0], ROW_TILE):
        x = x_ref[rows, :] + jnp.dot(a_ref[rows, :], w_ref[0], preferred_element_type=F32)
        o_ref[rows, :] = x
        if emit_stats:
            xb_ref[rows, :] = x.astype(BF16)
            ssq_ref[rows, :] = _lane_partial_sumsq(x)


def _resid_proj(a, w, x, layer, k_block, emit_stats, name):
    bm = BM // 2
    n_j, n_i = D_MODEL // BN, TOKENS // bm
    out_specs = [pl.BlockSpec((bm, BN), lambda j, i: (i, j))]
    out_shape = [jax.ShapeDtypeStruct((TOKENS, D_MODEL), F32)]
    if emit_stats:
        out_specs += [pl.BlockSpec((bm, BN), lambda j, i: (i, j)),
                      pl.BlockSpec((None, bm, LANES), lambda j, i: (j, i, 0))]
        out_shape += [jax.ShapeDtypeStruct((TOKENS, D_MODEL), BF16),
                      jax.ShapeDtypeStruct((n_j, TOKENS, LANES), F32)]
    return pl.pallas_call(
        functools.partial(_resid_kernel, emit_stats=emit_stats, layer=layer,
                          row0=k_block * D_MODEL, n_j=n_j, n_i=n_i),
        grid=(n_j, n_i),
        in_specs=[pl.BlockSpec((bm, D_MODEL), lambda j, i: (i, k_block)),
                  pl.BlockSpec((bm, BN), lambda j, i: (i, j)),
                  _HBM],
        out_specs=out_specs,
        out_shape=out_shape,
        scratch_shapes=_stream_scratch(1, D_MODEL, BN, n_i),
        compiler_params=_params(_SEQUENTIAL),
        name=name,
    )(a, x, w)


def _ffn_up_kernel(xb_ref, ssq_ref, gain_ref, cw_ref, cb_ref, w_hbm, o_ref, wbuf_ref, stage_ref, sem_ref,
                   abuf_ref, *, n_j, **stream):
    w_ref = _stream_weights([w_hbm, w_hbm], [0, n_j], wbuf_ref, stage_ref, sem_ref,
                            row0=0, n_j=n_j, row_gain_ref=gain_ref, **stream)
    i = pl.program_id(1)

    @pl.when(i % (SEQ // BM) == 0)
    def _():
        abuf_ref[0:8, :] = jnp.zeros((8, abuf_ref.shape[1]), F32)

    xb = xb_ref[...]
    inv = _row_inv_rms(ssq_ref, slice(0, BM))
    a = jnp.dot(xb, w_ref[0], preferred_element_type=F32) * inv
    abuf_ref[8:BM + 8, :] = a
    a1 = abuf_ref[7:BM + 7, :]
    a2 = abuf_ref[6:BM + 6, :]
    conv = a2 * cw_ref[0:1, :] + a1 * cw_ref[1:2, :] + a * cw_ref[2:3, :] + cb_ref[...]
    b = jnp.dot(xb, w_ref[1], preferred_element_type=F32) * inv
    o_ref[...] = (jax.nn.gelu(conv) * b).astype(o_ref.dtype)
    abuf_ref[0:8, :] = abuf_ref[BM:BM + 8, :]


def _ffn_up(xb, ssq, gain, w_up, conv_w, conv_b, layer):
    n_j = D_FF // FFN_BN
    return pl.pallas_call(
        functools.partial(_ffn_up_kernel, layer=layer, n_j=n_j, n_i=N_I),
        grid=(n_j, N_I),
        in_specs=_norm_in_specs(ssq.shape[0], layer) + [
            pl.BlockSpec((None, 3, FFN_BN), lambda j, i: (layer, 0, j)),
            pl.BlockSpec((None, 1, FFN_BN), lambda j, i: (layer, 0, j)),
            _HBM],
        out_specs=pl.BlockSpec((BM, FFN_BN), lambda j, i: (i, j)),
        out_shape=jax.ShapeDtypeStruct((TOKENS, D_FF), BF16),
        scratch_shapes=_stream_scratch(2, D_MODEL, FFN_BN, N_I) + [pltpu.VMEM((BM + 8, FFN_BN), F32)],
        compiler_params=_params(_SEQUENTIAL),
        name="ffn_up",
    )(xb, ssq, gain, conv_w, conv_b.reshape(DEPTH, 1, D_FF), w_up)


def kernel(x, rel_bias_table, norm_mix_g, w_in, conv_a_w, conv_a_b, w_branch_a, w_branch_b,
           w_o, norm_ffn_g, w_up, conv_f_w, conv_f_b, w_down, norm_final_g):
    bias = _bias_blocks(rel_bias_table)
    gain_mix = jnp.broadcast_to(norm_mix_g[:, :, None], (DEPTH, D_MODEL, LANES))
    gain_ffn = jnp.broadcast_to(norm_ffn_g[:, :, None], (DEPTH, D_MODEL, LANES))
    xs = x.reshape(TOKENS, D_MODEL)
    xb, ssq = _row_stats(xs)
    for l in range(DEPTH):
        pm = _in_proj(xb, ssq, gain_mix, w_in, l, 0, MAIN_COLS, False, "in_proj_main")
        gates = _in_proj(xb, ssq, gain_mix, w_in, l, MAIN_COLS // BN, GATE_COLS, True, "in_proj_gates")
        yb = _attention(pm, bias)
        ya = _conv_mixer(pm, conv_a_w, conv_a_b, l)
        merged = _mix(ya, yb, gates, w_branch_a, w_branch_b, l)
        xs, xb, ssq = _resid_proj(merged, w_o, xs, l, 0, True, "out_proj")
        g = _ffn_up(xb, ssq, gain_ffn, w_up, conv_f_w, conv_f_b, l)
        (xs,) = _resid_proj(g, w_down, xs, l, 0, False, "ffn_down")
        xs, xb, ssq = _resid_proj(g, w_down, xs, l, 1, True, "ffn_down")
    out = _rmsnorm(xs, norm_final_g, F32)
    return out.reshape(BATCH, SEQ, D_MODEL)
```

```python
import functools
import math

import jax
import jax.numpy as jnp
import numpy as np
from jax import lax
from jax.experimental import pallas as pl
from jax.experimental.pallas import tpu as pltpu

D_MODEL = 4096
BATCH = 4
SEQ = 2048
DEPTH = 4
TOKENS = BATCH * SEQ

CONV_WIDTH = D_MODEL // 4
HEAD_DIM = 128
GROUPS = ((128, 1), (512, 4), (2048, 16))
HEADS_PER_GROUP = 8
N_ATTN_HEADS = HEADS_PER_GROUP * len(GROUPS)
ATTN_WIDTH = N_ATTN_HEADS * HEAD_DIM
MERGED_ATTN_WIDTH = HEADS_PER_GROUP * HEAD_DIM
BLOCK = 128
NUM_BUCKETS = 32
MAX_DISTANCE = 2048
D_FF = 2 * D_MODEL
EPS = 1e-6
NEG = -1e30

MAIN_COLS = 3 * CONV_WIDTH + 3 * ATTN_WIDTH
GATE_COLS = 2 * D_MODEL
Q_COL_BLOCK = 3 * CONV_WIDTH // HEAD_DIM
N_ATTN_BLOCKS = SEQ // BLOCK
assert all(window // dil == BLOCK for window, dil in GROUPS)

VMEM_LIMIT_BYTES = 60000 * 1024

BF16 = jnp.bfloat16
F32 = jnp.float32


def _params(semantics):
    return pltpu.CompilerParams(dimension_semantics=semantics,
                                vmem_limit_bytes=VMEM_LIMIT_BYTES)


def _rmsnorm_kernel(x_ref, g_ref, o_ref):
    x = x_ref[...]
    inv = lax.rsqrt(jnp.mean(x * x, axis=-1, keepdims=True) + EPS)
    o_ref[...] = (x * inv * g_ref[...]).astype(o_ref.dtype)


def _rmsnorm(x, g, out_dtype):
    bm = 512
    return pl.pallas_call(
        _rmsnorm_kernel,
        grid=(TOKENS // bm,),
        in_specs=[pl.BlockSpec((bm, D_MODEL), lambda i: (i, 0)),
                  pl.BlockSpec((1, D_MODEL), lambda i: (0, 0))],
        out_specs=pl.BlockSpec((bm, D_MODEL), lambda i: (i, 0)),
        out_shape=jax.ShapeDtypeStruct((TOKENS, D_MODEL), out_dtype),
        compiler_params=_params(("parallel",)),
        name="rmsnorm",
    )(x, g.reshape(1, D_MODEL))


def _row_stats_kernel(x_ref, xb_ref, ssq_ref):
    x = x_ref[...]
    xb_ref[...] = x.astype(BF16)
    ssq_ref[...] = _lane_partial_sumsq(x)


def _row_stats(x):
    bm = 512
    return pl.pallas_call(
        _row_stats_kernel,
        grid=(TOKENS // bm,),
        in_specs=[pl.BlockSpec((bm, D_MODEL), lambda i: (i, 0))],
        out_specs=[pl.BlockSpec((bm, D_MODEL), lambda i: (i, 0)),
                   pl.BlockSpec((None, bm, LANES), lambda i: (0, i, 0))],
        out_shape=[jax.ShapeDtypeStruct((TOKENS, D_MODEL), BF16),
                   jax.ShapeDtypeStruct((1, TOKENS, LANES), F32)],
        compiler_params=_params(("parallel",)),
        name="row_stats",
    )(x)


def _stream_weights(body, w_refs, col_blocks, wbufs, stage_ref, sem_ref, *, layer, row0, n_j, n_i,
                    row_gain_ref=None):
    n_parts, k, bn = wbufs[0].shape
    ck = k // n_i
    j = pl.program_id(0)
    t = j * n_i + pl.program_id(1)

    def copies(u, slot):
        block = (u // n_i) % n_j
        rows = pl.ds(row0 + (u % n_i) * ck, ck)
        return [pltpu.make_async_copy(w.at[layer, rows, pl.ds((cb + block) * bn, bn)],
                                      stage_ref.at[slot, p], sem_ref.at[slot, p])
                for p, (w, cb) in enumerate(zip(w_refs, col_blocks))]

    def cast(u, slot, dst_ref):
        rows = pl.ds(pl.multiple_of((u % n_i) * ck, ck), ck)
        for p in range(n_parts):
            w = stage_ref[slot, p]
            if row_gain_ref is not None:
                w = w * jnp.concatenate([row_gain_ref[rows, :]] * (bn // LANES), axis=1)
            dst_ref[p, rows, :] = w.astype(BF16)

    @pl.when(t == 0)
    def _():
        for cp in copies(0, 0):
            cp.start()
        for c in range(n_i):
            for cp in copies(c + 1, (c + 1) % 2):
                cp.start()
            for cp in copies(c, c % 2):
                cp.wait()
            cast(c, c % 2, wbufs[0])

    u = t + n_i
    slot = u % 2
    for cp in copies(u, slot):
        cp.wait()

    @pl.when(t + 1 < n_j * n_i)
    def _():
        for cp in copies(u + 1, 1 - slot):
            cp.start()

    for parity in range(2):
        @pl.when(j % 2 == parity)
        def _():
            cast(u, slot, wbufs[1 - parity])
            body(wbufs[parity])


def _stream_scratch(n_parts, k, bn, n_i):
    return [pltpu.VMEM((n_parts, k, bn), BF16),
            pltpu.VMEM((n_parts, k, bn), BF16),
            pltpu.VMEM((2, n_parts, k // n_i, bn), F32),
            pltpu.SemaphoreType.DMA((2, n_parts))]


def _row_tiles(n_rows, tile):
    return [slice(r, r + tile) for r in range(0, n_rows, tile)]


def _row_inv_rms(ssq_ref, rows):
    tot = ssq_ref[0, rows, :]
    for p in range(1, ssq_ref.shape[0]):
        tot = tot + ssq_ref[p, rows, :]
    return lax.rsqrt(jnp.sum(tot, axis=-1, keepdims=True) * (1.0 / D_MODEL) + EPS)


def _lane_partial_sumsq(x):
    sq = x * x
    tot = sq[:, 0:LANES]
    for c in range(1, x.shape[1] // LANES):
        tot = tot + sq[:, c * LANES:(c + 1) * LANES]
    return tot


_HBM = pl.BlockSpec(memory_space=pl.ANY)
_SEQUENTIAL = ("arbitrary", "arbitrary")
BM = 1024
BN = 1024
FFN_BN = 512
N_I = TOKENS // BM
ROW_TILE = 256
LANES = 128


def _proj_kernel(xb_ref, ssq_ref, gain_ref, w_hbm, o_ref, wbuf0_ref, wbuf1_ref, stage_ref, sem_ref,
                 *, sigmoid, slab_out, col_block0, **stream):
    def body(w_ref):
        for rows in _row_tiles(BM, ROW_TILE):
            acc = jnp.dot(xb_ref[rows, :], w_ref[0], preferred_element_type=F32)
            acc = acc * _row_inv_rms(ssq_ref, rows)
            if sigmoid:
                acc = jax.nn.sigmoid(acc)
            if slab_out:
                for c in range(BN // LANES):
                    o_ref[c, rows, :] = acc[:, c * LANES:(c + 1) * LANES].astype(o_ref.dtype)
            else:
                o_ref[rows, :] = acc.astype(o_ref.dtype)

    _stream_weights(body, [w_hbm], [col_block0], (wbuf0_ref, wbuf1_ref), stage_ref, sem_ref, row0=0,
                    row_gain_ref=gain_ref, **stream)


def _norm_in_specs(n_ssq_parts, layer):
    return [pl.BlockSpec((BM, D_MODEL), lambda j, i: (i, 0)),
            pl.BlockSpec((n_ssq_parts, BM, LANES), lambda j, i: (0, i, 0)),
            pl.BlockSpec((None, D_MODEL, LANES), lambda j, i: (layer, 0, 0),
                         pipeline_mode=pl.Buffered(1))]


def _in_proj(xb, ssq, gain, w_in, layer, col_block0, n_cols, sigmoid, slab_out, name):
    n_j = n_cols // BN
    if slab_out:
        out_specs = pl.BlockSpec((BN // LANES, BM, LANES), lambda j, i: (j, i, 0))
        out_shape = jax.ShapeDtypeStruct((n_cols // LANES, TOKENS, LANES), BF16)
    else:
        out_specs = pl.BlockSpec((BM, BN), lambda j, i: (i, j))
        out_shape = jax.ShapeDtypeStruct((TOKENS, n_cols), BF16)
    return pl.pallas_call(
        functools.partial(_proj_kernel, sigmoid=sigmoid, slab_out=slab_out, layer=layer,
                          col_block0=col_block0, n_j=n_j, n_i=N_I),
        grid=(n_j, N_I),
        in_specs=_norm_in_specs(ssq.shape[0], layer) + [_HBM],
        out_specs=out_specs,
        out_shape=out_shape,
        scratch_shapes=_stream_scratch(1, D_MODEL, BN, N_I),
        compiler_params=_params(_SEQUENTIAL),
        name=name,
    )(xb, ssq, gain, w_in)


def _attn_kernel(*refs):
    qkv_refs = refs[:9]
    bias_ref = refs[9]
    o_ref = refs[10]
    stage_ref, stage2_ref, sub_ref, p_ref, on_ref, ln_ref = refs[11:]
    scale = HEAD_DIM ** -0.5

    row = lax.broadcasted_iota(jnp.int32, (BLOCK, BLOCK), 0)
    lane = lax.broadcasted_iota(jnp.int32, (BLOCK, BLOCK), 1)
    valid_cur = lane <= row
    valid_band = jnp.concatenate([lane >= row, valid_cur], axis=1)

    for g, (_, dil) in enumerate(GROUPS):
        sub_len = SEQ // dil
        blocks_per_sub = sub_len // BLOCK

        def to_sub(part):
            stage = stage_ref.at[g - 1, part]
            dst = sub_ref.at[g - 1, part]
            stage[...] = qkv_refs[3 * g + part][...].astype(F32)
            if dil == 4:
                for r in range(4):
                    rows = stage[pl.ds(r, sub_len, stride=4), :]
                    dst[r * sub_len:(r + 1) * sub_len, :] = rows.astype(BF16)
            else:
                stage2 = stage2_ref.at[part]
                quarter = SEQ // 4
                for b in range(4):
                    stage2[b * quarter:(b + 1) * quarter, :] = stage[pl.ds(b, quarter, stride=4), :]
                for b in range(4):
                    for a in range(4):
                        rows = stage2[pl.ds(b * quarter + a, sub_len, stride=4), :]
                        r = 4 * a + b
                        dst[r * sub_len:(r + 1) * sub_len, :] = rows.astype(BF16)
            return dst

        if dil == 1:
            qs_ref, ks_ref, vs_ref = qkv_refs[0:3]
        else:
            qs_ref, ks_ref, vs_ref = to_sub(0), to_sub(1), to_sub(2)

        for blk in range(N_ATTN_BLOCKS):
            r, n = divmod(blk, blocks_per_sub)
            row0 = blk * BLOCK
            qb = qs_ref[row0:row0 + BLOCK, :]
            if n > 0:
                kb = ks_ref[row0 - BLOCK:row0 + BLOCK, :]
                s = lax.dot_general(qb, kb, (((1,), (1,)), ((), ())),
                                    preferred_element_type=F32) * scale
                s = jnp.where(valid_band, s + bias_ref[g], NEG)
                m = jnp.max(s, axis=-1, keepdims=True)
                p = jnp.exp(s - m)
                den = jnp.sum(p, axis=-1, keepdims=True)
            else:
                kb = ks_ref[row0:row0 + BLOCK, :]
                s = lax.dot_general(qb, kb, (((1,), (1,)), ((), ())),
                                    preferred_element_type=F32) * scale
                s = jnp.where(valid_cur, s + bias_ref[g, :, BLOCK:], NEG)
                m = jnp.maximum(jnp.max(s, axis=-1, keepdims=True), NEG)
                p = jnp.exp(s - m)
                den = jnp.sum(p, axis=-1, keepdims=True) + BLOCK * jnp.exp(NEG - m)
            p_ref[g, row0:row0 + BLOCK, 0:p.shape[1]] = (p / den).astype(BF16)
            lse = jnp.broadcast_to(m + jnp.log(den), (BLOCK, HEAD_DIM))
            if dil == 1:
                ln_ref[g, row0:row0 + BLOCK, :] = lse
            else:
                ln_ref[g, pl.ds(n * BLOCK * dil + r, BLOCK, stride=dil), :] = lse

        for blk in range(N_ATTN_BLOCKS):
            r, n = divmod(blk, blocks_per_sub)
            row0 = blk * BLOCK
            if n > 0:
                o = jnp.dot(p_ref[g, row0:row0 + BLOCK, :], vs_ref[row0 - BLOCK:row0 + BLOCK, :],
                            preferred_element_type=F32)
            else:
                o = jnp.dot(p_ref[g, row0:row0 + BLOCK, 0:BLOCK], vs_ref[row0:row0 + BLOCK, :],
                            preferred_element_type=F32)
            if dil == 1:
                on_ref[g, row0:row0 + BLOCK, :] = o
            else:
                on_ref[g, pl.ds(n * BLOCK * dil + r, BLOCK, stride=dil), :] = o

    l0, l1, l2 = ln_ref[0], ln_ref[1], ln_ref[2]
    mx = jnp.maximum(jnp.maximum(l0, l1), l2)
    e0, e1, e2 = jnp.exp(l0 - mx), jnp.exp(l1 - mx), jnp.exp(l2 - mx)
    tot = e0 + e1 + e2
    out = (e0 / tot) * on_ref[0] + (e1 / tot) * on_ref[1] + (e2 / tot) * on_ref[2]
    o_ref[...] = out.astype(o_ref.dtype)


def _attention(pm, bias):
    in_specs = []
    for g in range(len(GROUPS)):
        for part in range(3):
            col0 = Q_COL_BLOCK + part * N_ATTN_HEADS + g * HEADS_PER_GROUP
            in_specs.append(pl.BlockSpec(
                (None, SEQ, HEAD_DIM), lambda b, h, col0=col0: (col0 + h, b, 0)))
    in_specs.append(pl.BlockSpec((len(GROUPS), None, BLOCK, 2 * BLOCK),
                                 lambda b, h: (0, h, 0, 0)))
    return pl.pallas_call(
        _attn_kernel,
        grid=(BATCH, HEADS_PER_GROUP),
        in_specs=in_specs,
        out_specs=pl.BlockSpec((SEQ, HEAD_DIM), lambda b, h: (b, h)),
        out_shape=jax.ShapeDtypeStruct((TOKENS, MERGED_ATTN_WIDTH), BF16),
        scratch_shapes=[
            pltpu.VMEM((2, 3, SEQ, HEAD_DIM), F32),
            pltpu.VMEM((3, SEQ, HEAD_DIM), F32),
            pltpu.VMEM((2, 3, SEQ, HEAD_DIM), BF16),
            pltpu.VMEM((len(GROUPS), SEQ, 2 * BLOCK), BF16),
            pltpu.VMEM((len(GROUPS), SEQ, HEAD_DIM), F32),
            pltpu.VMEM((len(GROUPS), SEQ, HEAD_DIM), F32),
        ],
        compiler_params=_params(("parallel", "parallel")),
        name="dilated_attention",
    )(*([pm] * 9), bias)


def _t5_bucket(dist):
    max_exact = NUM_BUCKETS // 2
    distf = jnp.maximum(dist, 1).astype(F32)
    large = max_exact + (jnp.log(distf / max_exact) / math.log(MAX_DISTANCE / max_exact)
                         * (NUM_BUCKETS - max_exact)).astype(jnp.int32)
    large = jnp.minimum(large, NUM_BUCKETS - 1)
    return jnp.where(dist < max_exact, dist, large)


def _bias_blocks(rel_bias_table):
    q_idx = jnp.arange(BLOCK)
    k_idx = jnp.arange(2 * BLOCK)
    delta = (q_idx[:, None] + BLOCK) - k_idx[None, :]
    out = []
    for g, (window, dil) in enumerate(GROUPS):
        steps = window // dil
        table_g = rel_bias_table[:, g * HEADS_PER_GROUP:(g + 1) * HEADS_PER_GROUP]
        bias_delta = table_g[_t5_bucket(jnp.arange(steps + 1) * dil)].astype(F32)
        onehot = (jnp.clip(delta, 0, steps)[:, :, None] == jnp.arange(steps + 1)).astype(F32)
        out.append(jnp.einsum('qkj,jh->hqk', onehot, bias_delta, precision=lax.Precision.HIGHEST))
    return jnp.stack(out, 0)


def _conv_mixer_kernel(ah_ref, ab_ref, ac_ref, ahp_ref, acp_ref, cw_ref, cb_ref, o_ref, ubuf_ref, *, bm):
    first = pl.program_id(0) % (SEQ // bm) == 0
    for c in range(CONV_WIDTH // LANES):
        cols = slice(c * LANES, (c + 1) * LANES)
        u = ac_ref[c].astype(F32) * ah_ref[c].astype(F32)
        halo = acp_ref[c].astype(F32) * ahp_ref[c].astype(F32)
        halo = jnp.where(first, 0.0, halo)
        ubuf_ref[c, 0:8, :] = halo[8:16, :]
        ubuf_ref[c, 8:bm + 8, :] = u
        u1 = ubuf_ref[c, 7:bm + 7, :]
        u2 = ubuf_ref[c, 6:bm + 6, :]
        conv = u2 * cw_ref[0:1, cols] + u1 * cw_ref[1:2, cols] + u * cw_ref[2:3, cols] + cb_ref[:, cols]
        o_ref[:, cols] = (ab_ref[c].astype(F32) * conv).astype(o_ref.dtype)


def _conv_mixer(pm, conv_w, conv_b, layer):
    bm = 512
    halo = 16
    n_slabs = CONV_WIDTH // LANES

    def prev_rows(part):
        return lambda i: (part, jnp.maximum(i * (bm // halo) - 1, 0), 0)

    return pl.pallas_call(
        functools.partial(_conv_mixer_kernel, bm=bm),
        grid=(TOKENS // bm,),
        in_specs=[
            pl.BlockSpec((n_slabs, bm, LANES), lambda i: (0, i, 0)),
            pl.BlockSpec((n_slabs, bm, LANES), lambda i: (1, i, 0)),
            pl.BlockSpec((n_slabs, bm, LANES), lambda i: (2, i, 0)),
            pl.BlockSpec((n_slabs, halo, LANES), prev_rows(0)),
            pl.BlockSpec((n_slabs, halo, LANES), prev_rows(2)),
            pl.BlockSpec((None, 3, CONV_WIDTH), lambda i: (layer, 0, 0)),
            pl.BlockSpec((None, 1, CONV_WIDTH), lambda i: (layer, 0, 0)),
        ],
        out_specs=pl.BlockSpec((bm, CONV_WIDTH), lambda i: (i, 0)),
        out_shape=jax.ShapeDtypeStruct((TOKENS, CONV_WIDTH), BF16),
        scratch_shapes=[pltpu.VMEM((n_slabs, bm + 8, LANES), F32)],
        compiler_params=_params(("parallel",)),
        name="conv_mixer",
    )(pm, pm, pm, pm, pm, conv_w, conv_b.reshape(DEPTH, 1, CONV_WIDTH))


def _mix_kernel(ya_ref, yb_ref, sa_ref, sb_ref, wa_hbm, wb_hbm, o_ref, wbuf0_ref, wbuf1_ref, stage_ref,
                sem_ref, **stream):
    def body(w_ref):
        for rows in _row_tiles(BM, ROW_TILE):
            br_a = jnp.dot(ya_ref[rows, :], w_ref[0], preferred_element_type=F32)
            br_b = jnp.dot(yb_ref[rows, :], w_ref[1], preferred_element_type=F32)
            merged = sa_ref[rows, :].astype(F32) * br_a + sb_ref[rows, :].astype(F32) * br_b
            o_ref[rows, :] = merged.astype(o_ref.dtype)

    _stream_weights(body, [wa_hbm, wb_hbm], [0, 0], (wbuf0_ref, wbuf1_ref), stage_ref, sem_ref,
                    row0=0, **stream)


def _mix(ya, yb, gates, w_a, w_b, layer):
    n_j = D_MODEL // BN
    return pl.pallas_call(
        functools.partial(_mix_kernel, layer=layer, n_j=n_j, n_i=N_I),
        grid=(n_j, N_I),
        in_specs=[
            pl.BlockSpec((BM, CONV_WIDTH), lambda j, i: (i, 0)),
            pl.BlockSpec((BM, MERGED_ATTN_WIDTH), lambda j, i: (i, 0)),
            pl.BlockSpec((BM, BN), lambda j, i: (i, j)),
            pl.BlockSpec((BM, BN), lambda j, i: (i, n_j + j)),
            _HBM, _HBM,
        ],
        out_specs=pl.BlockSpec((BM, BN), lambda j, i: (i, j)),
        out_shape=jax.ShapeDtypeStruct((TOKENS, D_MODEL), BF16),
        scratch_shapes=_stream_scratch(2, CONV_WIDTH, BN, N_I),
        compiler_params=_params(_SEQUENTIAL),
        name="mix_branches",
    )(ya, yb, gates, gates, w_a, w_b)


def _resid_kernel(a_ref, x_ref, w_hbm, o_ref, *rest, emit_stats, **stream):
    if emit_stats:
        xb_ref, ssq_ref, wbuf0_ref, wbuf1_ref, stage_ref, sem_ref = rest
    else:
        wbuf0_ref, wbuf1_ref, stage_ref, sem_ref = rest

    def body(w_ref):
        for rows in _row_tiles(a_ref.shape[0], ROW_TILE):
            x = x_ref[rows, :] + jnp.dot(a_ref[rows, :], w_ref[0], preferred_element_type=F32)
            o_ref[rows, :] = x
            if emit_stats:
                xb_ref[rows, :] = x.astype(BF16)
                ssq_ref[rows, :] = _lane_partial_sumsq(x)

    _stream_weights(body, [w_hbm], [0], (wbuf0_ref, wbuf1_ref), stage_ref, sem_ref, **stream)


def _resid_proj(a, w, x, layer, k_block, emit_stats, name):
    bm = BM // 2
    n_j, n_i = D_MODEL // BN, TOKENS // bm
    out_specs = [pl.BlockSpec((bm, BN), lambda j, i: (i, j))]
    out_shape = [jax.ShapeDtypeStruct((TOKENS, D_MODEL), F32)]
    if emit_stats:
        out_specs += [pl.BlockSpec((bm, BN), lambda j, i: (i, j)),
                      pl.BlockSpec((None, bm, LANES), lambda j, i: (j, i, 0))]
        out_shape += [jax.ShapeDtypeStruct((TOKENS, D_MODEL), BF16),
                      jax.ShapeDtypeStruct((n_j, TOKENS, LANES), F32)]
    return pl.pallas_call(
        functools.partial(_resid_kernel, emit_stats=emit_stats, layer=layer,
                          row0=k_block * D_MODEL, n_j=n_j, n_i=n_i),
        grid=(n_j, n_i),
        in_specs=[pl.BlockSpec((bm, D_MODEL), lambda j, i: (i, k_block)),
                  pl.BlockSpec((bm, BN), lambda j, i: (i, j)),
                  _HBM],
        out_specs=out_specs,
        out_shape=out_shape,
        scratch_shapes=_stream_scratch(1, D_MODEL, BN, n_i),
        compiler_params=_params(_SEQUENTIAL),
        name=name,
    )(a, x, w)


def _ffn_up_kernel(xb_ref, ssq_ref, gain_ref, cw_ref, cb_ref, w_hbm, o_ref, wbuf0_ref, wbuf1_ref, stage_ref,
                   sem_ref, abuf_ref, *, n_j, **stream):
    i = pl.program_id(1)

    @pl.when(i % (SEQ // BM) == 0)
    def _():
        abuf_ref[0:8, :] = jnp.zeros((8, abuf_ref.shape[1]), F32)

    def body(w_ref):
        xb = xb_ref[...]
        inv = _row_inv_rms(ssq_ref, slice(0, BM))
        a = jnp.dot(xb, w_ref[0], preferred_element_type=F32) * inv
        abuf_ref[8:BM + 8, :] = a
        a1 = abuf_ref[7:BM + 7, :]
        a2 = abuf_ref[6:BM + 6, :]
        conv = a2 * cw_ref[0:1, :] + a1 * cw_ref[1:2, :] + a * cw_ref[2:3, :] + cb_ref[...]
        b = jnp.dot(xb, w_ref[1], preferred_element_type=F32) * inv
        o_ref[...] = (jax.nn.gelu(conv) * b).astype(o_ref.dtype)
        abuf_ref[0:8, :] = abuf_ref[BM:BM + 8, :]

    _stream_weights(body, [w_hbm, w_hbm], [0, n_j], (wbuf0_ref, wbuf1_ref), stage_ref, sem_ref,
                    row0=0, n_j=n_j, row_gain_ref=gain_ref, **stream)


def _ffn_up(xb, ssq, gain, w_up, conv_w, conv_b, layer):
    n_j = D_FF // FFN_BN
    return pl.pallas_call(
        functools.partial(_ffn_up_kernel, layer=layer, n_j=n_j, n_i=N_I),
        grid=(n_j, N_I),
        in_specs=_norm_in_specs(ssq.shape[0], layer) + [
            pl.BlockSpec((None, 3, FFN_BN), lambda j, i: (layer, 0, j)),
            pl.BlockSpec((None, 1, FFN_BN), lambda j, i: (layer, 0, j)),
            _HBM],
        out_specs=pl.BlockSpec((BM, FFN_BN), lambda j, i: (i, j)),
        out_shape=jax.ShapeDtypeStruct((TOKENS, D_FF), BF16),
        scratch_shapes=_stream_scratch(2, D_MODEL, FFN_BN, N_I) + [pltpu.VMEM((BM + 8, FFN_BN), F32)],
        compiler_params=_params(_SEQUENTIAL),
        name="ffn_up",
    )(xb, ssq, gain, conv_w, conv_b.reshape(DEPTH, 1, D_FF), w_up)


def kernel(x, rel_bias_table, norm_mix_g, w_in, conv_a_w, conv_a_b, w_branch_a, w_branch_b,
           w_o, norm_ffn_g, w_up, conv_f_w, conv_f_b, w_down, norm_final_g):
    bias = _bias_blocks(rel_bias_table)
    gain_mix = jnp.broadcast_to(norm_mix_g[:, :, None], (DEPTH, D_MODEL, LANES))
    gain_ffn = jnp.broadcast_to(norm_ffn_g[:, :, None], (DEPTH, D_MODEL, LANES))
    xs = x.reshape(TOKENS, D_MODEL)
    xb, ssq = _row_stats(xs)
    for l in range(DEPTH):
        pm = _in_proj(xb, ssq, gain_mix, w_in, l, 0, MAIN_COLS, False, True, "in_proj_main")
        gates = _in_proj(xb, ssq, gain_mix, w_in, l, MAIN_COLS // BN, GATE_COLS, True, False, "in_proj_gates")
        yb = _attention(pm, bias)
        ya = _conv_mixer(pm, conv_a_w, conv_a_b, l)
        merged = _mix(ya, yb, gates, w_branch_a, w_branch_b, l)
        xs, xb, ssq = _resid_proj(merged, w_o, xs, l, 0, True, "out_proj")
        g = _ffn_up(xb, ssq, gain_ffn, w_up, conv_f_w, conv_f_b, l)
        (xs,) = _resid_proj(g, w_down, xs, l, 0, False, "ffn_down")
        xs, xb, ssq = _resid_proj(g, w_down, xs, l, 1, True, "ffn_down")
    out = _rmsnorm(xs, norm_final_g, F32)
    return out.reshape(BATCH, SEQ, D_MODEL)
```

```python
import functools
import math

import jax
import jax.numpy as jnp
import numpy as np
from jax import lax
from jax.experimental import pallas as pl
from jax.experimental.pallas import tpu as pltpu

D_MODEL = 4096
BATCH = 4
SEQ = 2048
DEPTH = 4
TOKENS = BATCH * SEQ

CONV_WIDTH = D_MODEL // 4
HEAD_DIM = 128
GROUPS = ((128, 1), (512, 4), (2048, 16))
HEADS_PER_GROUP = 8
N_ATTN_HEADS = HEADS_PER_GROUP * len(GROUPS)
ATTN_WIDTH = N_ATTN_HEADS * HEAD_DIM
MERGED_ATTN_WIDTH = HEADS_PER_GROUP * HEAD_DIM
BLOCK = 128
NUM_BUCKETS = 32
MAX_DISTANCE = 2048
D_FF = 2 * D_MODEL
EPS = 1e-6
NEG = -1e30

MAIN_COLS = 3 * CONV_WIDTH + 3 * ATTN_WIDTH
GATE_COLS = 2 * D_MODEL
Q_COL_BLOCK = 3 * CONV_WIDTH // HEAD_DIM
N_ATTN_BLOCKS = SEQ // BLOCK
assert all(window // dil == BLOCK for window, dil in GROUPS)

VMEM_LIMIT_BYTES = 60000 * 1024

BF16 = jnp.bfloat16
F32 = jnp.float32


def _params(semantics):
    return pltpu.CompilerParams(dimension_semantics=semantics,
                                vmem_limit_bytes=VMEM_LIMIT_BYTES)


def _rmsnorm_kernel(x_ref, g_ref, o_ref):
    x = x_ref[...]
    inv = lax.rsqrt(jnp.mean(x * x, axis=-1, keepdims=True) + EPS)
    o_ref[...] = (x * inv * g_ref[...]).astype(o_ref.dtype)


def _rmsnorm(x, g, out_dtype):
    bm = 512
    return pl.pallas_call(
        _rmsnorm_kernel,
        grid=(TOKENS // bm,),
        in_specs=[pl.BlockSpec((bm, D_MODEL), lambda i: (i, 0)),
                  pl.BlockSpec((1, D_MODEL), lambda i: (0, 0))],
        out_specs=pl.BlockSpec((bm, D_MODEL), lambda i: (i, 0)),
        out_shape=jax.ShapeDtypeStruct((TOKENS, D_MODEL), out_dtype),
        compiler_params=_params(("parallel",)),
        name="rmsnorm",
    )(x, g.reshape(1, D_MODEL))


def _row_stats_kernel(x_ref, g_ref, xb_ref, ssq_ref):
    x = x_ref[...]
    xb_ref[...] = (x * g_ref[...]).astype(BF16)
    ssq_ref[...] = _lane_partial_sumsq(x)


def _row_stats(x, g):
    bm = 512
    return pl.pallas_call(
        _row_stats_kernel,
        grid=(TOKENS // bm,),
        in_specs=[pl.BlockSpec((bm, D_MODEL), lambda i: (i, 0)),
                  pl.BlockSpec((1, D_MODEL), lambda i: (0, 0))],
        out_specs=[pl.BlockSpec((bm, D_MODEL), lambda i: (i, 0)),
                   pl.BlockSpec((None, bm, LANES), lambda i: (0, i, 0))],
        out_shape=[jax.ShapeDtypeStruct((TOKENS, D_MODEL), BF16),
                   jax.ShapeDtypeStruct((1, TOKENS, LANES), F32)],
        compiler_params=_params(("parallel",)),
        name="row_stats",
    )(x, g.reshape(1, D_MODEL))


def _row_tiles(n_rows, tile):
    return [slice(r, r + tile) for r in range(0, n_rows, tile)]


def _row_inv_rms(ssq_ref, rows):
    tot = ssq_ref[0, rows, :]
    for p in range(1, ssq_ref.shape[0]):
        tot = tot + ssq_ref[p, rows, :]
    return lax.rsqrt(jnp.sum(tot, axis=-1, keepdims=True) * (1.0 / D_MODEL) + EPS)


def _lane_partial_sumsq(x):
    sq = x * x
    tot = sq[:, 0:LANES]
    for c in range(1, x.shape[1] // LANES):
        tot = tot + sq[:, c * LANES:(c + 1) * LANES]
    return tot


BM = 1024
BN = 1024
FFN_BN = 512
N_I = TOKENS // BM
ROW_TILE = 256
LANES = 128


def _proj_kernel(xb_ref, ssq_ref, w_ref, o_ref, *, sigmoid):
    for rows in _row_tiles(BM, ROW_TILE):
        acc = jnp.dot(xb_ref[rows, :], w_ref[...], preferred_element_type=F32)
        acc = acc * _row_inv_rms(ssq_ref, rows)
        if sigmoid:
            acc = jax.nn.sigmoid(acc)
        o_ref[rows, :] = acc.astype(o_ref.dtype)


def _norm_in_specs(n_ssq_parts):
    return [pl.BlockSpec((BM, D_MODEL), lambda j, i: (i, 0)),
            pl.BlockSpec((n_ssq_parts, BM, LANES), lambda j, i: (0, i, 0))]


def _in_proj(xb, ssq, w_in, layer, col_block0, n_cols, sigmoid, name):
    n_j = n_cols // BN
    return pl.pallas_call(
        functools.partial(_proj_kernel, sigmoid=sigmoid),
        grid=(n_j, N_I),
        in_specs=_norm_in_specs(ssq.shape[0]) + [
            pl.BlockSpec((None, D_MODEL, BN), lambda j, i: (layer, 0, col_block0 + j))],
        out_specs=pl.BlockSpec((BM, BN), lambda j, i: (i, j)),
        out_shape=jax.ShapeDtypeStruct((TOKENS, n_cols), BF16),
        compiler_params=_params(("parallel", "parallel")),
        name=name,
    )(xb, ssq, w_in)


def _attn_kernel(*refs):
    qkv_refs = refs[:9]
    bias_ref = refs[9]
    o_ref = refs[10]
    stage_ref, stage2_ref, sub_ref, p_ref, on_ref, ln_ref = refs[11:]
    scale = HEAD_DIM ** -0.5

    row = lax.broadcasted_iota(jnp.int32, (BLOCK, BLOCK), 0)
    lane = lax.broadcasted_iota(jnp.int32, (BLOCK, BLOCK), 1)
    valid_cur = lane <= row
    valid_band = jnp.concatenate([lane >= row, valid_cur], axis=1)

    for g, (_, dil) in enumerate(GROUPS):
        sub_len = SEQ // dil
        blocks_per_sub = sub_len // BLOCK

        def to_sub(part):
            stage = stage_ref.at[g - 1, part]
            dst = sub_ref.at[g - 1, part]
            stage[...] = qkv_refs[3 * g + part][...].astype(F32)
            if dil == 4:
                for r in range(4):
                    rows = stage[pl.ds(r, sub_len, stride=4), :]
                    dst[r * sub_len:(r + 1) * sub_len, :] = rows.astype(BF16)
            else:
                stage2 = stage2_ref.at[part]
                quarter = SEQ // 4
                for b in range(4):
                    stage2[b * quarter:(b + 1) * quarter, :] = stage[pl.ds(b, quarter, stride=4), :]
                for b in range(4):
                    for a in range(4):
                        rows = stage2[pl.ds(b * quarter + a, sub_len, stride=4), :]
                        r = 4 * a + b
                        dst[r * sub_len:(r + 1) * sub_len, :] = rows.astype(BF16)
            return dst

        if dil == 1:
            qs_ref, ks_ref, vs_ref = qkv_refs[0:3]
        else:
            qs_ref, ks_ref, vs_ref = to_sub(0), to_sub(1), to_sub(2)

        for blk in range(N_ATTN_BLOCKS):
            r, n = divmod(blk, blocks_per_sub)
            row0 = blk * BLOCK
            qb = qs_ref[row0:row0 + BLOCK, :]
            if n > 0:
                kb = ks_ref[row0 - BLOCK:row0 + BLOCK, :]
                s = lax.dot_general(qb, kb, (((1,), (1,)), ((), ())),
                                    preferred_element_type=F32) * scale
                s = jnp.where(valid_band, s + bias_ref[g], NEG)
                m = jnp.max(s, axis=-1, keepdims=True)
                p = jnp.exp(s - m)
                den = jnp.sum(p, axis=-1, keepdims=True)
            else:
                kb = ks_ref[row0:row0 + BLOCK, :]
                s = lax.dot_general(qb, kb, (((1,), (1,)), ((), ())),
                                    preferred_element_type=F32) * scale
                s = jnp.where(valid_cur, s + bias_ref[g, :, BLOCK:], NEG)
                m = jnp.maximum(jnp.max(s, axis=-1, keepdims=True), NEG)
                p = jnp.exp(s - m)
                den = jnp.sum(p, axis=-1, keepdims=True) + BLOCK * jnp.exp(NEG - m)
            p_ref[g, row0:row0 + BLOCK, 0:p.shape[1]] = (p / den).astype(BF16)
            lse = jnp.broadcast_to(m + jnp.log(den), (BLOCK, HEAD_DIM))
            if dil == 1:
                ln_ref[g, row0:row0 + BLOCK, :] = lse
            else:
                ln_ref[g, pl.ds(n * BLOCK * dil + r, BLOCK, stride=dil), :] = lse

        for blk in range(N_ATTN_BLOCKS):
            r, n = divmod(blk, blocks_per_sub)
            row0 = blk * BLOCK
            if n > 0:
                o = jnp.dot(p_ref[g, row0:row0 + BLOCK, :], vs_ref[row0 - BLOCK:row0 + BLOCK, :],
                            preferred_element_type=F32)
            else:
                o = jnp.dot(p_ref[g, row0:row0 + BLOCK, 0:BLOCK], vs_ref[row0:row0 + BLOCK, :],
                            preferred_element_type=F32)
            if dil == 1:
                on_ref[g, row0:row0 + BLOCK, :] = o
            else:
                on_ref[g, pl.ds(n * BLOCK * dil + r, BLOCK, stride=dil), :] = o

    l0, l1, l2 = ln_ref[0], ln_ref[1], ln_ref[2]
    mx = jnp.maximum(jnp.maximum(l0, l1), l2)
    e0, e1, e2 = jnp.exp(l0 - mx), jnp.exp(l1 - mx), jnp.exp(l2 - mx)
    tot = e0 + e1 + e2
    out = (e0 / tot) * on_ref[0] + (e1 / tot) * on_ref[1] + (e2 / tot) * on_ref[2]
    o_ref[...] = out.astype(o_ref.dtype)


def _attention(pm, bias):
    in_specs = []
    for g in range(len(GROUPS)):
        for part in range(3):
            col0 = Q_COL_BLOCK + part * N_ATTN_HEADS + g * HEADS_PER_GROUP
            in_specs.append(pl.BlockSpec(
                (SEQ, HEAD_DIM), lambda b, h, col0=col0: (b, col0 + h)))
    in_specs.append(pl.BlockSpec((len(GROUPS), None, BLOCK, 2 * BLOCK),
                                 lambda b, h: (0, h, 0, 0)))
    return pl.pallas_call(
        _attn_kernel,
        grid=(BATCH, HEADS_PER_GROUP),
        in_specs=in_specs,
        out_specs=pl.BlockSpec((SEQ, HEAD_DIM), lambda b, h: (b, h)),
        out_shape=jax.ShapeDtypeStruct((TOKENS, MERGED_ATTN_WIDTH), BF16),
        scratch_shapes=[
            pltpu.VMEM((2, 3, SEQ, HEAD_DIM), F32),
            pltpu.VMEM((3, SEQ, HEAD_DIM), F32),
            pltpu.VMEM((2, 3, SEQ, HEAD_DIM), BF16),
            pltpu.VMEM((len(GROUPS), SEQ, 2 * BLOCK), BF16),
            pltpu.VMEM((len(GROUPS), SEQ, HEAD_DIM), F32),
            pltpu.VMEM((len(GROUPS), SEQ, HEAD_DIM), F32),
        ],
        compiler_params=_params(("parallel", "parallel")),
        name="dilated_attention",
    )(*([pm] * 9), bias)


def _t5_bucket(dist):
    max_exact = NUM_BUCKETS // 2
    distf = jnp.maximum(dist, 1).astype(F32)
    large = max_exact + (jnp.log(distf / max_exact) / math.log(MAX_DISTANCE / max_exact)
                         * (NUM_BUCKETS - max_exact)).astype(jnp.int32)
    large = jnp.minimum(large, NUM_BUCKETS - 1)
    return jnp.where(dist < max_exact, dist, large)


def _bias_blocks(rel_bias_table):
    q_idx = jnp.arange(BLOCK)
    k_idx = jnp.arange(2 * BLOCK)
    delta = (q_idx[:, None] + BLOCK) - k_idx[None, :]
    out = []
    for g, (window, dil) in enumerate(GROUPS):
        steps = window // dil
        table_g = rel_bias_table[:, g * HEADS_PER_GROUP:(g + 1) * HEADS_PER_GROUP]
        bias_delta = table_g[_t5_bucket(jnp.arange(steps + 1) * dil)].astype(F32)
        onehot = (jnp.clip(delta, 0, steps)[:, :, None] == jnp.arange(steps + 1)).astype(F32)
        out.append(jnp.einsum('qkj,jh->hqk', onehot, bias_delta, precision=lax.Precision.HIGHEST))
    return jnp.stack(out, 0)


def _conv_mixer_kernel(ah_ref, ab_ref, ac_ref, ahp_ref, acp_ref, cw_ref, cb_ref, o_ref, ubuf_ref, *, bm):
    i = pl.program_id(0)
    u = ac_ref[...].astype(F32) * ah_ref[...].astype(F32)
    halo = acp_ref[...].astype(F32) * ahp_ref[...].astype(F32)
    halo = jnp.where(i % (SEQ // bm) == 0, 0.0, halo)
    ubuf_ref[0:8, :] = halo[8:16, :]
    ubuf_ref[8:bm + 8, :] = u
    u1 = ubuf_ref[7:bm + 7, :]
    u2 = ubuf_ref[6:bm + 6, :]
    conv = u2 * cw_ref[0:1, :] + u1 * cw_ref[1:2, :] + u * cw_ref[2:3, :] + cb_ref[...]
    o_ref[...] = (ab_ref[...].astype(F32) * conv).astype(o_ref.dtype)


def _conv_mixer(pm, conv_w, conv_b, layer):
    bm = 512
    halo = 16

    def prev_rows(col):
        return lambda i: (jnp.maximum(i * (bm // halo) - 1, 0), col)

    return pl.pallas_call(
        functools.partial(_conv_mixer_kernel, bm=bm),
        grid=(TOKENS // bm,),
        in_specs=[
            pl.BlockSpec((bm, CONV_WIDTH), lambda i: (i, 0)),
            pl.BlockSpec((bm, CONV_WIDTH), lambda i: (i, 1)),
            pl.BlockSpec((bm, CONV_WIDTH), lambda i: (i, 2)),
            pl.BlockSpec((halo, CONV_WIDTH), prev_rows(0)),
            pl.BlockSpec((halo, CONV_WIDTH), prev_rows(2)),
            pl.BlockSpec((None, 3, CONV_WIDTH), lambda i: (layer, 0, 0)),
            pl.BlockSpec((None, 1, CONV_WIDTH), lambda i: (layer, 0, 0)),
        ],
        out_specs=pl.BlockSpec((bm, CONV_WIDTH), lambda i: (i, 0)),
        out_shape=jax.ShapeDtypeStruct((TOKENS, CONV_WIDTH), BF16),
        scratch_shapes=[pltpu.VMEM((bm + 8, CONV_WIDTH), F32)],
        compiler_params=_params(("parallel",)),
        name="conv_mixer",
    )(pm, pm, pm, pm, pm, conv_w, conv_b.reshape(DEPTH, 1, CONV_WIDTH))


def _mix_kernel(ya_ref, yb_ref, sa_ref, sb_ref, wa_ref, wb_ref, o_ref):
    for rows in _row_tiles(BM, ROW_TILE):
        br_a = jnp.dot(ya_ref[rows, :], wa_ref[...], preferred_element_type=F32)
        br_b = jnp.dot(yb_ref[rows, :], wb_ref[...], preferred_element_type=F32)
        merged = sa_ref[rows, :].astype(F32) * br_a + sb_ref[rows, :].astype(F32) * br_b
        o_ref[rows, :] = merged.astype(o_ref.dtype)


def _mix(ya, yb, gates, w_a, w_b, layer):
    n_j = D_MODEL // BN
    return pl.pallas_call(
        _mix_kernel,
        grid=(n_j, N_I),
        in_specs=[
            pl.BlockSpec((BM, CONV_WIDTH), lambda j, i: (i, 0)),
            pl.BlockSpec((BM, MERGED_ATTN_WIDTH), lambda j, i: (i, 0)),
            pl.BlockSpec((BM, BN), lambda j, i: (i, j)),
            pl.BlockSpec((BM, BN), lambda j, i: (i, n_j + j)),
            pl.BlockSpec((None, CONV_WIDTH, BN), lambda j, i: (layer, 0, j)),
            pl.BlockSpec((None, MERGED_ATTN_WIDTH, BN), lambda j, i: (layer, 0, j)),
        ],
        out_specs=pl.BlockSpec((BM, BN), lambda j, i: (i, j)),
        out_shape=jax.ShapeDtypeStruct((TOKENS, D_MODEL), BF16),
        compiler_params=_params(("parallel", "parallel")),
        name="mix_branches",
    )(ya, yb, gates, gates, w_a, w_b)


def _resid_kernel(a_ref, x_ref, w_ref, *refs, emit_stats):
    if emit_stats:
        g_ref, o_ref, xb_ref, ssq_ref = refs
    else:
        (o_ref,) = refs
    for rows in _row_tiles(a_ref.shape[0], ROW_TILE):
        x = x_ref[rows, :] + jnp.dot(a_ref[rows, :], w_ref[...], preferred_element_type=F32)
        o_ref[rows, :] = x
        if emit_stats:
            xb_ref[rows, :] = (x * g_ref[...]).astype(BF16)
            ssq_ref[rows, :] = _lane_partial_sumsq(x)


def _resid_proj(a, w, x, layer, k_block, next_gain, name):
    bm = BM // 2
    n_j, n_i = D_MODEL // BN, TOKENS // bm
    emit_stats = next_gain is not None
    in_specs = [pl.BlockSpec((bm, D_MODEL), lambda j, i: (i, k_block)),
                pl.BlockSpec((bm, BN), lambda j, i: (i, j)),
                pl.BlockSpec((None, D_MODEL, BN), lambda j, i: (layer, k_block, j))]
    operands = [a, x, w]
    out_specs = [pl.BlockSpec((bm, BN), lambda j, i: (i, j))]
    out_shape = [jax.ShapeDtypeStruct((TOKENS, D_MODEL), F32)]
    if emit_stats:
        gains, gain_index = next_gain
        in_specs.append(pl.BlockSpec((None, 1, BN), lambda j, i: (gain_index, 0, j)))
        operands.append(gains.reshape(DEPTH, 1, D_MODEL))
        out_specs += [pl.BlockSpec((bm, BN), lambda j, i: (i, j)),
                      pl.BlockSpec((None, bm, LANES), lambda j, i: (j, i, 0))]
        out_shape += [jax.ShapeDtypeStruct((TOKENS, D_MODEL), BF16),
                      jax.ShapeDtypeStruct((n_j, TOKENS, LANES), F32)]
    return pl.pallas_call(
        functools.partial(_resid_kernel, emit_stats=emit_stats),
        grid=(n_j, n_i),
        in_specs=in_specs,
        out_specs=out_specs,
        out_shape=out_shape,
        compiler_params=_params(("parallel", "parallel")),
        name=name,
    )(*operands)


def _stream_weights(w_refs, col_blocks, wbuf_ref, stage_ref, sem_ref, *, layer, n_j, n_i):
    _, n_parts, k, bn = wbuf_ref.shape
    ck = k // n_i
    j = pl.program_id(0)
    t = j * n_i + pl.program_id(1)

    def copies(u, slot):
        block = (u // n_i) % n_j
        rows = pl.ds((u % n_i) * ck, ck)
        return [pltpu.make_async_copy(w.at[layer, rows, pl.ds((cb + block) * bn, bn)],
                                      stage_ref.at[slot, p], sem_ref.at[slot, p])
                for p, (w, cb) in enumerate(zip(w_refs, col_blocks))]

    def cast(u, slot):
        rows = pl.ds(pl.multiple_of((u % n_i) * ck, ck), ck)
        for p in range(n_parts):
            wbuf_ref[(u // n_i) % 2, p, rows, :] = stage_ref[slot, p].astype(BF16)

    @pl.when(t == 0)
    def _():
        for cp in copies(0, 0):
            cp.start()
        for c in range(n_i - 1):
            for cp in copies(c + 1, (c + 1) % 2):
                cp.start()
            for cp in copies(c, c % 2):
                cp.wait()
            cast(c, c % 2)

    u = t + n_i - 1
    slot = u % 2
    for cp in copies(u, slot):
        cp.wait()

    @pl.when(t + 1 < n_j * n_i)
    def _():
        for cp in copies(u + 1, 1 - slot):
            cp.start()

    cast(u, slot)
    return wbuf_ref.at[j % 2]


def _ffn_up_kernel(xb_ref, ssq_ref, cw_ref, cb_ref, w_hbm, o_ref, wbuf_ref, stage_ref, sem_ref, abuf_ref,
                   **stream):
    w_ref = _stream_weights([w_hbm, w_hbm], [0, stream["n_j"]], wbuf_ref, stage_ref, sem_ref, **stream)
    i = pl.program_id(1)

    @pl.when(i % (SEQ // BM) == 0)
    def _():
        abuf_ref[0:8, :] = jnp.zeros((8, abuf_ref.shape[1]), F32)

    xb = xb_ref[...]
    inv = _row_inv_rms(ssq_ref, slice(0, BM))
    a = jnp.dot(xb, w_ref[0], preferred_element_type=F32) * inv
    abuf_ref[8:BM + 8, :] = a
    a1 = abuf_ref[7:BM + 7, :]
    a2 = abuf_ref[6:BM + 6, :]
    conv = a2 * cw_ref[0:1, :] + a1 * cw_ref[1:2, :] + a * cw_ref[2:3, :] + cb_ref[...]
    b = jnp.dot(xb, w_ref[1], preferred_element_type=F32) * inv
    o_ref[...] = (jax.nn.gelu(conv) * b).astype(o_ref.dtype)
    abuf_ref[0:8, :] = abuf_ref[BM:BM + 8, :]


def _ffn_up(xb, ssq, w_up, conv_w, conv_b, layer):
    n_j = D_FF // FFN_BN
    return pl.pallas_call(
        functools.partial(_ffn_up_kernel, layer=layer, n_j=n_j, n_i=N_I),
        grid=(n_j, N_I),
        in_specs=_norm_in_specs(ssq.shape[0]) + [
            pl.BlockSpec((None, 3, FFN_BN), lambda j, i: (layer, 0, j)),
            pl.BlockSpec((None, 1, FFN_BN), lambda j, i: (layer, 0, j)),
            pl.BlockSpec(memory_space=pl.ANY)],
        out_specs=pl.BlockSpec((BM, FFN_BN), lambda j, i: (i, j)),
        out_shape=jax.ShapeDtypeStruct((TOKENS, D_FF), BF16),
        scratch_shapes=[pltpu.VMEM((2, 2, D_MODEL, FFN_BN), BF16),
                        pltpu.VMEM((2, 2, D_MODEL // N_I, FFN_BN), F32),
                        pltpu.SemaphoreType.DMA((2, 2)),
                        pltpu.VMEM((BM + 8, FFN_BN), F32)],
        compiler_params=_params(("arbitrary", "arbitrary")),
        name="ffn_up",
    )(xb, ssq, conv_w, conv_b.reshape(DEPTH, 1, D_FF), w_up)


def kernel(x, rel_bias_table, norm_mix_g, w_in, conv_a_w, conv_a_b, w_branch_a, w_branch_b,
           w_o, norm_ffn_g, w_up, conv_f_w, conv_f_b, w_down, norm_final_g):
    bias = _bias_blocks(rel_bias_table)
    xs = x.reshape(TOKENS, D_MODEL)
    xb, ssq = _row_stats(xs, norm_mix_g[0])
    for l in range(DEPTH):
        pm = _in_proj(xb, ssq, w_in, l, 0, MAIN_COLS, False, "in_proj_main")
        gates = _in_proj(xb, ssq, w_in, l, MAIN_COLS // BN, GATE_COLS, True, "in_proj_gates")
        yb = _attention(pm, bias)
        ya = _conv_mixer(pm, conv_a_w, conv_a_b, l)
        merged = _mix(ya, yb, gates, w_branch_a, w_branch_b, l)
        xs, xb, ssq = _resid_proj(merged, w_o, xs, l, 0, (norm_ffn_g, l), "out_proj")
        g = _ffn_up(xb, ssq, w_up, conv_f_w, conv_f_b, l)
        (xs,) = _resid_proj(g, w_down, xs, l, 0, None, "ffn_down")
        if l + 1 < DEPTH:
            xs, xb, ssq = _resid_proj(g, w_down, xs, l, 1, (norm_mix_g, l + 1), "ffn_down")
        else:
            (xs,) = _resid_proj(g, w_down, xs, l, 1, None, "ffn_down")
    out = _rmsnorm(xs, norm_final_g, F32)
    return out.reshape(BATCH, SEQ, D_MODEL)
```

```python
import functools
import math

import jax
import jax.numpy as jnp
import numpy as np
from jax import lax
from jax.experimental import pallas as pl
from jax.experimental.pallas import tpu as pltpu

D_MODEL = 4096
BATCH = 4
SEQ = 2048
DEPTH = 4
TOKENS = BATCH * SEQ

CONV_WIDTH = D_MODEL // 4
HEAD_DIM = 128
GROUPS = ((128, 1), (512, 4), (2048, 16))
HEADS_PER_GROUP = 8
N_ATTN_HEADS = HEADS_PER_GROUP * len(GROUPS)
ATTN_WIDTH = N_ATTN_HEADS * HEAD_DIM
MERGED_ATTN_WIDTH = HEADS_PER_GROUP * HEAD_DIM
BLOCK = 128
NUM_BUCKETS = 32
MAX_DISTANCE = 2048
D_FF = 2 * D_MODEL
EPS = 1e-6
NEG = -1e30

MAIN_COLS = 3 * CONV_WIDTH + 3 * ATTN_WIDTH
GATE_COLS = 2 * D_MODEL
Q_COL_BLOCK = 3 * CONV_WIDTH // HEAD_DIM
N_ATTN_BLOCKS = SEQ // BLOCK
assert all(window // dil == BLOCK for window, dil in GROUPS)

VMEM_LIMIT_BYTES = 60000 * 1024

BF16 = jnp.bfloat16
F32 = jnp.float32


def _params(semantics):
    return pltpu.CompilerParams(dimension_semantics=semantics,
                                vmem_limit_bytes=VMEM_LIMIT_BYTES)


def _rmsnorm_kernel(x_ref, g_ref, o_ref):
    x = x_ref[...]
    inv = lax.rsqrt(jnp.mean(x * x, axis=-1, keepdims=True) + EPS)
    o_ref[...] = (x * inv * g_ref[...]).astype(o_ref.dtype)


def _rmsnorm(x, g, out_dtype):
    bm = 512
    return pl.pallas_call(
        _rmsnorm_kernel,
        grid=(TOKENS // bm,),
        in_specs=[pl.BlockSpec((bm, D_MODEL), lambda i: (i, 0)),
                  pl.BlockSpec((1, D_MODEL), lambda i: (0, 0))],
        out_specs=pl.BlockSpec((bm, D_MODEL), lambda i: (i, 0)),
        out_shape=jax.ShapeDtypeStruct((TOKENS, D_MODEL), out_dtype),
        compiler_params=_params(("parallel",)),
        name="rmsnorm",
    )(x, g.reshape(1, D_MODEL))


def _row_stats_kernel(x_ref, g_ref, xb_ref, ssq_ref):
    x = x_ref[...]
    xb_ref[...] = (x * g_ref[...]).astype(BF16)
    ssq_ref[...] = _lane_partial_sumsq(x)


def _row_stats(x, g):
    bm = 512
    return pl.pallas_call(
        _row_stats_kernel,
        grid=(TOKENS // bm,),
        in_specs=[pl.BlockSpec((bm, D_MODEL), lambda i: (i, 0)),
                  pl.BlockSpec((1, D_MODEL), lambda i: (0, 0))],
        out_specs=[pl.BlockSpec((bm, D_MODEL), lambda i: (i, 0)),
                   pl.BlockSpec((None, bm, LANES), lambda i: (0, i, 0))],
        out_shape=[jax.ShapeDtypeStruct((TOKENS, D_MODEL), BF16),
                   jax.ShapeDtypeStruct((1, TOKENS, LANES), F32)],
        compiler_params=_params(("parallel",)),
        name="row_stats",
    )(x, g.reshape(1, D_MODEL))


def _row_tiles(n_rows, tile):
    return [slice(r, r + tile) for r in range(0, n_rows, tile)]


def _row_inv_rms(ssq_ref, rows):
    tot = ssq_ref[0, rows, :]
    for p in range(1, ssq_ref.shape[0]):
        tot = tot + ssq_ref[p, rows, :]
    return lax.rsqrt(jnp.sum(tot, axis=-1, keepdims=True) * (1.0 / D_MODEL) + EPS)


def _lane_partial_sumsq(x):
    sq = x * x
    tot = sq[:, 0:LANES]
    for c in range(1, x.shape[1] // LANES):
        tot = tot + sq[:, c * LANES:(c + 1) * LANES]
    return tot


BM = 1024
BN = 1024
FFN_BN = 512
N_I = TOKENS // BM
ROW_TILE = 256
LANES = 128


def _proj_kernel(xb_ref, ssq_ref, w_ref, o_ref, *, sigmoid):
    for rows in _row_tiles(BM, ROW_TILE):
        acc = jnp.dot(xb_ref[rows, :], w_ref[...], preferred_element_type=F32)
        acc = acc * _row_inv_rms(ssq_ref, rows)
        if sigmoid:
            acc = jax.nn.sigmoid(acc)
        o_ref[rows, :] = acc.astype(o_ref.dtype)


def _norm_in_specs(n_ssq_parts):
    return [pl.BlockSpec((BM, D_MODEL), lambda j, i: (i, 0)),
            pl.BlockSpec((n_ssq_parts, BM, LANES), lambda j, i: (0, i, 0))]


def _in_proj(xb, ssq, w_in, layer, col_block0, n_cols, sigmoid, name):
    n_j = n_cols // BN
    return pl.pallas_call(
        functools.partial(_proj_kernel, sigmoid=sigmoid),
        grid=(n_j, N_I),
        in_specs=_norm_in_specs(ssq.shape[0]) + [
            pl.BlockSpec((None, D_MODEL, BN), lambda j, i: (layer, 0, col_block0 + j))],
        out_specs=pl.BlockSpec((BM, BN), lambda j, i: (i, j)),
        out_shape=jax.ShapeDtypeStruct((TOKENS, n_cols), BF16),
        compiler_params=_params(("parallel", "parallel")),
        name=name,
    )(xb, ssq, w_in)


def _attn_kernel(*refs):
    qkv_refs = refs[:9]
    bias_ref = refs[9]
    o_ref = refs[10]
    stage_ref, stage2_ref, sub_ref, p_ref, on_ref, ln_ref = refs[11:]
    scale = HEAD_DIM ** -0.5

    row = lax.broadcasted_iota(jnp.int32, (BLOCK, BLOCK), 0)
    lane = lax.broadcasted_iota(jnp.int32, (BLOCK, BLOCK), 1)
    valid_cur = lane <= row
    valid_band = jnp.concatenate([lane >= row, valid_cur], axis=1)

    for g, (_, dil) in enumerate(GROUPS):
        sub_len = SEQ // dil
        blocks_per_sub = sub_len // BLOCK

        def to_sub(part):
            stage = stage_ref.at[g - 1, part]
            dst = sub_ref.at[g - 1, part]
            stage[...] = qkv_refs[3 * g + part][...].astype(F32)
            if dil == 4:
                for r in range(4):
                    rows = stage[pl.ds(r, sub_len, stride=4), :]
                    dst[r * sub_len:(r + 1) * sub_len, :] = rows.astype(BF16)
            else:
                stage2 = stage2_ref.at[part]
                quarter = SEQ // 4
                for b in range(4):
                    stage2[b * quarter:(b + 1) * quarter, :] = stage[pl.ds(b, quarter, stride=4), :]
                for b in range(4):
                    for a in range(4):
                        rows = stage2[pl.ds(b * quarter + a, sub_len, stride=4), :]
                        r = 4 * a + b
                        dst[r * sub_len:(r + 1) * sub_len, :] = rows.astype(BF16)
            return dst

        if dil == 1:
            qs_ref, ks_ref, vs_ref = qkv_refs[0:3]
        else:
            qs_ref, ks_ref, vs_ref = to_sub(0), to_sub(1), to_sub(2)

        for blk in range(N_ATTN_BLOCKS):
            r, n = divmod(blk, blocks_per_sub)
            row0 = blk * BLOCK
            qb = qs_ref[row0:row0 + BLOCK, :]
            if n > 0:
                kb = ks_ref[row0 - BLOCK:row0 + BLOCK, :]
                s = lax.dot_general(qb, kb, (((1,), (1,)), ((), ())),
                                    preferred_element_type=F32) * scale
                s = jnp.where(valid_band, s + bias_ref[g], NEG)
                m = jnp.max(s, axis=-1, keepdims=True)
                p = jnp.exp(s - m)
                den = jnp.sum(p, axis=-1, keepdims=True)
            else:
                kb = ks_ref[row0:row0 + BLOCK, :]
                s = lax.dot_general(qb, kb, (((1,), (1,)), ((), ())),
                                    preferred_element_type=F32) * scale
                s = jnp.where(valid_cur, s + bias_ref[g, :, BLOCK:], NEG)
                m = jnp.maximum(jnp.max(s, axis=-1, keepdims=True), NEG)
                p = jnp.exp(s - m)
                den = jnp.sum(p, axis=-1, keepdims=True) + BLOCK * jnp.exp(NEG - m)
            p_ref[g, row0:row0 + BLOCK, 0:p.shape[1]] = (p / den).astype(BF16)
            lse = jnp.broadcast_to(m + jnp.log(den), (BLOCK, HEAD_DIM))
            if dil == 1:
                ln_ref[g, row0:row0 + BLOCK, :] = lse
            else:
                ln_ref[g, pl.ds(n * BLOCK * dil + r, BLOCK, stride=dil), :] = lse

        for blk in range(N_ATTN_BLOCKS):
            r, n = divmod(blk, blocks_per_sub)
            row0 = blk * BLOCK
            if n > 0:
                o = jnp.dot(p_ref[g, row0:row0 + BLOCK, :], vs_ref[row0 - BLOCK:row0 + BLOCK, :],
                            preferred_element_type=F32)
            else:
                o = jnp.dot(p_ref[g, row0:row0 + BLOCK, 0:BLOCK], vs_ref[row0:row0 + BLOCK, :],
                            preferred_element_type=F32)
            if dil == 1:
                on_ref[g, row0:row0 + BLOCK, :] = o
            else:
                on_ref[g, pl.ds(n * BLOCK * dil + r, BLOCK, stride=dil), :] = o

    l0, l1, l2 = ln_ref[0], ln_ref[1], ln_ref[2]
    mx = jnp.maximum(jnp.maximum(l0, l1), l2)
    e0, e1, e2 = jnp.exp(l0 - mx), jnp.exp(l1 - mx), jnp.exp(l2 - mx)
    tot = e0 + e1 + e2
    out = (e0 / tot) * on_ref[0] + (e1 / tot) * on_ref[1] + (e2 / tot) * on_ref[2]
    o_ref[...] = out.astype(o_ref.dtype)


def _attention(pm, bias):
    in_specs = []
    for g in range(len(GROUPS)):
        for part in range(3):
            col0 = Q_COL_BLOCK + part * N_ATTN_HEADS + g * HEADS_PER_GROUP
            in_specs.append(pl.BlockSpec(
                (SEQ, HEAD_DIM), lambda b, h, col0=col0: (b, col0 + h)))
    in_specs.append(pl.BlockSpec((len(GROUPS), None, BLOCK, 2 * BLOCK),
                                 lambda b, h: (0, h, 0, 0)))
    return pl.pallas_call(
        _attn_kernel,
        grid=(BATCH, HEADS_PER_GROUP),
        in_specs=in_specs,
        out_specs=pl.BlockSpec((SEQ, HEAD_DIM), lambda b, h: (b, h)),
        out_shape=jax.ShapeDtypeStruct((TOKENS, MERGED_ATTN_WIDTH), BF16),
        scratch_shapes=[
            pltpu.VMEM((2, 3, SEQ, HEAD_DIM), F32),
            pltpu.VMEM((3, SEQ, HEAD_DIM), F32),
            pltpu.VMEM((2, 3, SEQ, HEAD_DIM), BF16),
            pltpu.VMEM((len(GROUPS), SEQ, 2 * BLOCK), BF16),
            pltpu.VMEM((len(GROUPS), SEQ, HEAD_DIM), F32),
            pltpu.VMEM((len(GROUPS), SEQ, HEAD_DIM), F32),
        ],
        compiler_params=_params(("parallel", "parallel")),
        name="dilated_attention",
    )(*([pm] * 9), bias)


def _t5_bucket(dist):
    max_exact = NUM_BUCKETS // 2
    distf = jnp.maximum(dist, 1).astype(F32)
    large = max_exact + (jnp.log(distf / max_exact) / math.log(MAX_DISTANCE / max_exact)
                         * (NUM_BUCKETS - max_exact)).astype(jnp.int32)
    large = jnp.minimum(large, NUM_BUCKETS - 1)
    return jnp.where(dist < max_exact, dist, large)


def _bias_blocks(rel_bias_table):
    q_idx = jnp.arange(BLOCK)
    k_idx = jnp.arange(2 * BLOCK)
    delta = (q_idx[:, None] + BLOCK) - k_idx[None, :]
    out = []
    for g, (window, dil) in enumerate(GROUPS):
        steps = window // dil
        table_g = rel_bias_table[:, g * HEADS_PER_GROUP:(g + 1) * HEADS_PER_GROUP]
        bias_delta = table_g[_t5_bucket(jnp.arange(steps + 1) * dil)].astype(F32)
        onehot = (jnp.clip(delta, 0, steps)[:, :, None] == jnp.arange(steps + 1)).astype(F32)
        out.append(jnp.einsum('qkj,jh->hqk', onehot, bias_delta, precision=lax.Precision.HIGHEST))
    return jnp.stack(out, 0)


def _conv_mixer_kernel(ah_ref, ab_ref, ac_ref, ahp_ref, acp_ref, cw_ref, cb_ref, o_ref, ubuf_ref, *, bm):
    i = pl.program_id(0)
    u = ac_ref[...].astype(F32) * ah_ref[...].astype(F32)
    halo = acp_ref[...].astype(F32) * ahp_ref[...].astype(F32)
    halo = jnp.where(i % (SEQ // bm) == 0, 0.0, halo)
    ubuf_ref[0:8, :] = halo[8:16, :]
    ubuf_ref[8:bm + 8, :] = u
    u1 = ubuf_ref[7:bm + 7, :]
    u2 = ubuf_ref[6:bm + 6, :]
    conv = u2 * cw_ref[0:1, :] + u1 * cw_ref[1:2, :] + u * cw_ref[2:3, :] + cb_ref[...]
    o_ref[...] = (ab_ref[...].astype(F32) * conv).astype(o_ref.dtype)


def _conv_mixer(pm, conv_w, conv_b, layer):
    bm = 512
    halo = 16

    def prev_rows(col):
        return lambda i: (jnp.maximum(i * (bm // halo) - 1, 0), col)

    return pl.pallas_call(
        functools.partial(_conv_mixer_kernel, bm=bm),
        grid=(TOKENS // bm,),
        in_specs=[
            pl.BlockSpec((bm, CONV_WIDTH), lambda i: (i, 0)),
            pl.BlockSpec((bm, CONV_WIDTH), lambda i: (i, 1)),
            pl.BlockSpec((bm, CONV_WIDTH), lambda i: (i, 2)),
            pl.BlockSpec((halo, CONV_WIDTH), prev_rows(0)),
            pl.BlockSpec((halo, CONV_WIDTH), prev_rows(2)),
            pl.BlockSpec((None, 3, CONV_WIDTH), lambda i: (layer, 0, 0)),
            pl.BlockSpec((None, 1, CONV_WIDTH), lambda i: (layer, 0, 0)),
        ],
        out_specs=pl.BlockSpec((bm, CONV_WIDTH), lambda i: (i, 0)),
        out_shape=jax.ShapeDtypeStruct((TOKENS, CONV_WIDTH), BF16),
        scratch_shapes=[pltpu.VMEM((bm + 8, CONV_WIDTH), F32)],
        compiler_params=_params(("parallel",)),
        name="conv_mixer",
    )(pm, pm, pm, pm, pm, conv_w, conv_b.reshape(DEPTH, 1, CONV_WIDTH))


def _mix_kernel(ya_ref, yb_ref, sa_ref, sb_ref, wa_ref, wb_ref, o_ref):
    for rows in _row_tiles(BM, ROW_TILE):
        br_a = jnp.dot(ya_ref[rows, :], wa_ref[...], preferred_element_type=F32)
        br_b = jnp.dot(yb_ref[rows, :], wb_ref[...], preferred_element_type=F32)
        merged = sa_ref[rows, :].astype(F32) * br_a + sb_ref[rows, :].astype(F32) * br_b
        o_ref[rows, :] = merged.astype(o_ref.dtype)


def _mix(ya, yb, gates, w_a, w_b, layer):
    n_j = D_MODEL // BN
    return pl.pallas_call(
        _mix_kernel,
        grid=(n_j, N_I),
        in_specs=[
            pl.BlockSpec((BM, CONV_WIDTH), lambda j, i: (i, 0)),
            pl.BlockSpec((BM, MERGED_ATTN_WIDTH), lambda j, i: (i, 0)),
            pl.BlockSpec((BM, BN), lambda j, i: (i, j)),
            pl.BlockSpec((BM, BN), lambda j, i: (i, n_j + j)),
            pl.BlockSpec((None, CONV_WIDTH, BN), lambda j, i: (layer, 0, j)),
            pl.BlockSpec((None, MERGED_ATTN_WIDTH, BN), lambda j, i: (layer, 0, j)),
        ],
        out_specs=pl.BlockSpec((BM, BN), lambda j, i: (i, j)),
        out_shape=jax.ShapeDtypeStruct((TOKENS, D_MODEL), BF16),
        compiler_params=_params(("parallel", "parallel")),
        name="mix_branches",
    )(ya, yb, gates, gates, w_a, w_b)


def _resid_kernel(a_ref, x_ref, w_hbm, *refs, emit_stats, **stream):
    if emit_stats:
        g_ref, o_ref, xb_ref, ssq_ref, wbuf_ref, stage_ref, sem_ref = refs
    else:
        o_ref, wbuf_ref, stage_ref, sem_ref = refs
    w_ref = _stream_weights([w_hbm], [0], wbuf_ref, stage_ref, sem_ref, **stream)
    for rows in _row_tiles(a_ref.shape[0], ROW_TILE):
        x = x_ref[rows, :] + jnp.dot(a_ref[rows, :], w_ref[0], preferred_element_type=F32)
        o_ref[rows, :] = x
        if emit_stats:
            xb_ref[rows, :] = (x * g_ref[...]).astype(BF16)
            ssq_ref[rows, :] = _lane_partial_sumsq(x)


def _resid_proj(a, w, x, layer, k_block, next_gain, name):
    bm = BM // 2
    n_j, n_i = D_MODEL // BN, TOKENS // bm
    emit_stats = next_gain is not None
    in_specs = [pl.BlockSpec((bm, D_MODEL), lambda j, i: (i, k_block)),
                pl.BlockSpec((bm, BN), lambda j, i: (i, j)),
                pl.BlockSpec(memory_space=pl.ANY)]
    operands = [a, x, w]
    out_specs = [pl.BlockSpec((bm, BN), lambda j, i: (i, j))]
    out_shape = [jax.ShapeDtypeStruct((TOKENS, D_MODEL), F32)]
    if emit_stats:
        gains, gain_index = next_gain
        in_specs.append(pl.BlockSpec((None, 1, BN), lambda j, i: (gain_index, 0, j)))
        operands.append(gains.reshape(DEPTH, 1, D_MODEL))
        out_specs += [pl.BlockSpec((bm, BN), lambda j, i: (i, j)),
                      pl.BlockSpec((None, bm, LANES), lambda j, i: (j, i, 0))]
        out_shape += [jax.ShapeDtypeStruct((TOKENS, D_MODEL), BF16),
                      jax.ShapeDtypeStruct((n_j, TOKENS, LANES), F32)]
    return pl.pallas_call(
        functools.partial(_resid_kernel, emit_stats=emit_stats, layer=layer,
                          row0=k_block * D_MODEL, n_j=n_j, n_i=n_i),
        grid=(n_j, n_i),
        in_specs=in_specs,
        out_specs=out_specs,
        out_shape=out_shape,
        scratch_shapes=[pltpu.VMEM((2, 1, D_MODEL, BN), BF16),
                        pltpu.VMEM((2, 1, D_MODEL // n_i, BN), F32),
                        pltpu.SemaphoreType.DMA((2, 1))],
        compiler_params=_params(("arbitrary", "arbitrary")),
        name=name,
    )(*operands)


def _stream_weights(w_refs, col_blocks, wbuf_ref, stage_ref, sem_ref, *, layer, n_j, n_i, row0=0):
    _, n_parts, k, bn = wbuf_ref.shape
    ck = k // n_i
    j = pl.program_id(0)
    t = j * n_i + pl.program_id(1)

    def copies(u, slot):
        block = (u // n_i) % n_j
        rows = pl.ds(row0 + (u % n_i) * ck, ck)
        return [pltpu.make_async_copy(w.at[layer, rows, pl.ds((cb + block) * bn, bn)],
                                      stage_ref.at[slot, p], sem_ref.at[slot, p])
                for p, (w, cb) in enumerate(zip(w_refs, col_blocks))]

    def cast(u, slot):
        rows = pl.ds(pl.multiple_of((u % n_i) * ck, ck), ck)
        for p in range(n_parts):
            wbuf_ref[(u // n_i) % 2, p, rows, :] = stage_ref[slot, p].astype(BF16)

    @pl.when(t == 0)
    def _():
        for cp in copies(0, 0):
            cp.start()
        for c in range(n_i - 1):
            for cp in copies(c + 1, (c + 1) % 2):
                cp.start()
            for cp in copies(c, c % 2):
                cp.wait()
            cast(c, c % 2)

    u = t + n_i - 1
    slot = u % 2
    for cp in copies(u, slot):
        cp.wait()

    @pl.when(t + 1 < n_j * n_i)
    def _():
        for cp in copies(u + 1, 1 - slot):
            cp.start()

    cast(u, slot)
    return wbuf_ref.at[j % 2]


def _ffn_up_kernel(xb_ref, ssq_ref, cw_ref, cb_ref, w_hbm, o_ref, wbuf_ref, stage_ref, sem_ref, abuf_ref,
                   **stream):
    w_ref = _stream_weights([w_hbm, w_hbm], [0, stream["n_j"]], wbuf_ref, stage_ref, sem_ref, **stream)
    i = pl.program_id(1)

    @pl.when(i % (SEQ // BM) == 0)
    def _():
        abuf_ref[0:8, :] = jnp.zeros((8, abuf_ref.shape[1]), F32)

    xb = xb_ref[...]
    inv = _row_inv_rms(ssq_ref, slice(0, BM))
    a = jnp.dot(xb, w_ref[0], preferred_element_type=F32) * inv
    abuf_ref[8:BM + 8, :] = a
    a1 = abuf_ref[7:BM + 7, :]
    a2 = abuf_ref[6:BM + 6, :]
    conv = a2 * cw_ref[0:1, :] + a1 * cw_ref[1:2, :] + a * cw_ref[2:3, :] + cb_ref[...]
    b = jnp.dot(xb, w_ref[1], preferred_element_type=F32) * inv
    o_ref[...] = (jax.nn.gelu(conv) * b).astype(o_ref.dtype)
    abuf_ref[0:8, :] = abuf_ref[BM:BM + 8, :]


def _ffn_up(xb, ssq, w_up, conv_w, conv_b, layer):
    n_j = D_FF // FFN_BN
    return pl.pallas_call(
        functools.partial(_ffn_up_kernel, layer=layer, n_j=n_j, n_i=N_I),
        grid=(n_j, N_I),
        in_specs=_norm_in_specs(ssq.shape[0]) + [
            pl.BlockSpec((None, 3, FFN_BN), lambda j, i: (layer, 0, j)),
            pl.BlockSpec((None, 1, FFN_BN), lambda j, i: (layer, 0, j)),
            pl.BlockSpec(memory_space=pl.ANY)],
        out_specs=pl.BlockSpec((BM, FFN_BN), lambda j, i: (i, j)),
        out_shape=jax.ShapeDtypeStruct((TOKENS, D_FF), BF16),
        scratch_shapes=[pltpu.VMEM((2, 2, D_MODEL, FFN_BN), BF16),
                        pltpu.VMEM((2, 2, D_MODEL // N_I, FFN_BN), F32),
                        pltpu.SemaphoreType.DMA((2, 2)),
                        pltpu.VMEM((BM + 8, FFN_BN), F32)],
        compiler_params=_params(("arbitrary", "arbitrary")),
        name="ffn_up",
    )(xb, ssq, conv_w, conv_b.reshape(DEPTH, 1, D_FF), w_up)


def kernel(x, rel_bias_table, norm_mix_g, w_in, conv_a_w, conv_a_b, w_branch_a, w_branch_b,
           w_o, norm_ffn_g, w_up, conv_f_w, conv_f_b, w_down, norm_final_g):
    bias = _bias_blocks(rel_bias_table)
    xs = x.reshape(TOKENS, D_MODEL)
    xb, ssq = _row_stats(xs, norm_mix_g[0])
    for l in range(DEPTH):
        pm = _in_proj(xb, ssq, w_in, l, 0, MAIN_COLS, False, "in_proj_main")
        gates = _in_proj(xb, ssq, w_in, l, MAIN_COLS // BN, GATE_COLS, True, "in_proj_gates")
        yb = _attention(pm, bias)
        ya = _conv_mixer(pm, conv_a_w, conv_a_b, l)
        merged = _mix(ya, yb, gates, w_branch_a, w_branch_b, l)
        xs, xb, ssq = _resid_proj(merged, w_o, xs, l, 0, (norm_ffn_g, l), "out_proj")
        g = _ffn_up(xb, ssq, w_up, conv_f_w, conv_f_b, l)
        (xs,) = _resid_proj(g, w_down, xs, l, 0, None, "ffn_down")
        if l + 1 < DEPTH:
            xs, xb, ssq = _resid_proj(g, w_down, xs, l, 1, (norm_mix_g, l + 1), "ffn_down")
        else:
            (xs,) = _resid_proj(g, w_down, xs, l, 1, None, "ffn_down")
    out = _rmsnorm(xs, norm_final_g, F32)
    return out.reshape(BATCH, SEQ, D_MODEL)
```
